```python
import math
import jax, jax.numpy as jnp
from jax import lax
import numpy as np

D_MODEL = 1024
BATCH = 16
SEQ = 2048
DEPTH = 4

EPS = 1e-6
N_MOD = 6
C_CONV = 512
CONV_K = 31
NSA_HEADS = 8
NSA_KV_GROUPS = 2
NSA_HEAD_DIM = 64
NSA_WIDTH = NSA_HEADS * NSA_HEAD_DIM
ROPE_DIM = NSA_HEAD_DIM // 4
ROPE_THETA = 500000.0
L_CMP = 32
STRIDE_CMP = 16
L_SLC = 64
N_SLC_TOP = 8
WINDOW = 512
WIN_Q_BLOCK = 128
SLC_Q_BLOCK = 64
GDN_HEADS = 4
GDN_DK = 128
GDN_DV = 128
GDN_WIDTH = GDN_HEADS * GDN_DV
GDN_QKV = GDN_HEADS * (2 * GDN_DK + GDN_DV)
GDN_CONV_K = 4
GDN_CHUNK = 64
FFN_HIDDEN = -(-8 * D_MODEL // (3 * 256)) * 256
IN_SPLITS = (2 * C_CONV, NSA_WIDTH, 6 * NSA_KV_GROUPS * NSA_HEAD_DIM, 3 * NSA_HEADS,
             GDN_QKV, GDN_WIDTH, GDN_HEADS, GDN_HEADS, 3 * D_MODEL)
N_IN = sum(IN_SPLITS)
IN_OFFSETS = tuple(int(v) for v in np.cumsum(IN_SPLITS)[:-1])

kernel_name = "hybrid_conv_nsa_gdn_adaln_trunk"


def rms_norm(x, g):
    xf = x.astype(jnp.float32)
    y = xf * lax.rsqrt(jnp.mean(xf * xf, axis=-1, keepdims=True) + EPS)
    return (y * g.astype(jnp.float32)).astype(x.dtype)


def layer_norm(x, g, b):
    xf = x.astype(jnp.float32)
    mu = jnp.mean(xf, axis=-1, keepdims=True)
    var = jnp.mean(jnp.square(xf - mu), axis=-1, keepdims=True)
    y = (xf - mu) * lax.rsqrt(var + EPS) * g.astype(jnp.float32) + b.astype(jnp.float32)
    return y.astype(x.dtype)


def masked_softmax(s, mask):
    s = jnp.where(mask, s.astype(jnp.float32), -1e30)
    m = jnp.max(s, axis=-1, keepdims=True)
    p = jnp.where(mask, jnp.exp(s - m), 0.0)
    return p / jnp.maximum(jnp.sum(p, axis=-1, keepdims=True), 1e-30)


def causal_depthwise_conv(x, w):
    k_len, ch = w.shape
    return lax.conv_general_dilated(
        x, w[:, None, :].astype(x.dtype), window_strides=(1,), padding=[(k_len - 1, 0)],
        dimension_numbers=('NWC', 'WIO', 'NWC'), feature_group_count=ch)


def partial_rope(x, pos):
    half = ROPE_DIM // 2
    inv = ROPE_THETA ** (-jnp.arange(half, dtype=jnp.float32) * 2.0 / ROPE_DIM)
    ang = pos.astype(jnp.float32)[..., None] * inv
    cos = jnp.cos(ang)[:, :, None, :]
    sin = jnp.sin(ang)[:, :, None, :]
    x1 = x[..., :half].astype(jnp.float32)
    x2 = x[..., half:ROPE_DIM].astype(jnp.float32)
    r1 = (x1 * cos - x2 * sin).astype(x.dtype)
    r2 = (x2 * cos + x1 * sin).astype(x.dtype)
    return jnp.concatenate([r1, r2, x[..., ROPE_DIM:]], axis=-1)


def conformer_conv(u, dw_w, dw_b, ln_g, ln_b):
    a, b = jnp.split(u, 2, axis=-1)
    y = a * jax.nn.sigmoid(b)
    y = causal_depthwise_conv(y, dw_w) + dw_b
    y = layer_norm(y, ln_g, ln_b)
    return jax.nn.silu(y)


def nsa_mixer(q, kv, gates, pos, pe_k, pe_v, wk_cmp, wv_cmp):
    bsz, seq = q.shape[0], q.shape[1]
    G, HPG, DH = NSA_KV_GROUPS, NSA_HEADS // NSA_KV_GROUPS, NSA_HEAD_DIM
    scale = DH ** -0.5
    kv = kv.reshape(bsz, seq, 6, G, DH)
    k_cmp, v_cmp, k_slc, v_slc, k_win, v_win = [kv[:, :, i] for i in range(6)]
    q = partial_rope(q, pos)
    k_cmp = partial_rope(k_cmp, pos)
    k_slc = partial_rope(k_slc, pos)
    k_win = partial_rope(k_win, pos)
    qh = q.reshape(bsz, seq, G, HPG, DH).transpose(0, 2, 3, 1, 4)
    t = jnp.arange(seq)

    n_cmp = (seq - L_CMP) // STRIDE_CMP + 1
    blk = jnp.arange(n_cmp)[:, None] * STRIDE_CMP + jnp.arange(L_CMP)
    kc = jnp.einsum('bnlgd,lde->bgne', k_cmp[:, blk] + pe_k[:, None, :], wk_cmp)
    vc = jnp.einsum('bnlgd,lde->bgne', v_cmp[:, blk] + pe_v[:, None, :], wv_cmp)
    cmp_end = jnp.arange(n_cmp) * STRIDE_CMP + L_CMP - 1
    cmp_mask = cmp_end[None, :] <= t[:, None]
    s_cmp = jnp.einsum('bghqd,bgnd->bghqn', qh, kc) * scale
    p_cmp = masked_softmax(s_cmp, cmp_mask)
    o_cmp = jnp.einsum('bghqn,bgnd->bghqd', p_cmp.astype(vc.dtype), vc)

    n_slc = seq // L_SLC
    cs = jnp.arange(n_cmp) * STRIDE_CMP
    js = jnp.arange(n_slc) * L_SLC
    overlap = ((cs[:, None] < js[None, :] + L_SLC) & (cs[:, None] + L_CMP > js[None, :])).astype(jnp.float32)
    imp = jnp.einsum('bghqn,nj->bgqj', p_cmp, overlap)
    cur = t // L_SLC
    jj = jnp.arange(n_slc)
    causal_blk = jj[None, :] <= cur[:, None]
    forced = (jj[None, :] == 0) | (jj[None, :] == cur[:, None]) | (jj[None, :] == cur[:, None] - 1)
    imp = jnp.where(forced, 1e9, jnp.where(causal_blk, imp, -1e9))
    n_top = min(N_SLC_TOP, n_slc)
    top_val, top_idx = lax.top_k(imp, n_top)
    top_ok = top_val > -1e8

    kh_slc = k_slc.transpose(0, 2, 1, 3)
    vh_slc = v_slc.transpose(0, 2, 1, 3)
    n_qc = seq // SLC_Q_BLOCK
    q_chunks = jnp.moveaxis(qh.reshape(bsz, G, HPG, n_qc, SLC_Q_BLOCK, DH), 3, 0)
    idx_chunks = jnp.moveaxis(top_idx.reshape(bsz, G, n_qc, SLC_Q_BLOCK, n_top), 2, 0)
    ok_chunks = jnp.moveaxis(top_ok.reshape(bsz, G, n_qc, SLC_Q_BLOCK, n_top), 2, 0)
    t_chunks = t.reshape(n_qc, SLC_Q_BLOCK)
    bi = jnp.arange(bsz)[:, None, None]
    gi = jnp.arange(G)[None, :, None]
    n_keys = n_top * L_SLC

    def slc_block(args):
        q_c, idx_c, ok_c, t_c = args
        tok = idx_c[..., None] * L_SLC + jnp.arange(L_SLC)
        flat = tok.reshape(bsz, G, -1)
        k_g = kh_slc[bi, gi, flat].reshape(bsz, G, SLC_Q_BLOCK, n_keys, DH)
        v_g = vh_slc[bi, gi, flat].reshape(bsz, G, SLC_Q_BLOCK, n_keys, DH)
        mask = ((tok <= t_c[:, None, None]) & ok_c[..., None]).reshape(bsz, G, SLC_Q_BLOCK, n_keys)
        s = jnp.einsum('bghqd,bgqkd->bghqk', q_c, k_g) * scale
        p = masked_softmax(s, mask[:, :, None])
        return jnp.einsum('bghqk,bgqkd->bghqd', p.astype(v_g.dtype), v_g)

    o_slc = lax.map(slc_block, (q_chunks, idx_chunks, ok_chunks, t_chunks))
    o_slc = jnp.moveaxis(o_slc, 0, 3).reshape(bsz, G, HPG, seq, DH)

    kh_win = jnp.pad(k_win.transpose(0, 2, 1, 3), ((0, 0), (0, 0), (WINDOW, 0), (0, 0)))
    vh_win = jnp.pad(v_win.transpose(0, 2, 1, 3), ((0, 0), (0, 0), (WINDOW, 0), (0, 0)))
    n_wb = seq // WIN_Q_BLOCK
    band = WINDOW + WIN_Q_BLOCK
    qw_chunks = jnp.moveaxis(qh.reshape(bsz, G, HPG, n_wb, WIN_Q_BLOCK, DH), 3, 0)

    def win_block(args):
        q_c, b = args
        start = b * WIN_Q_BLOCK
        k_b = lax.dynamic_slice_in_dim(kh_win, start, band, axis=2)
        v_b = lax.dynamic_slice_in_dim(vh_win, start, band, axis=2)
        qpos = start + jnp.arange(WIN_Q_BLOCK)
        kpos = start - WINDOW + jnp.arange(band)
        mask = ((kpos[None, :] <= qpos[:, None]) & (kpos[None, :] > qpos[:, None] - WINDOW)
                & (kpos[None, :] >= 0))
        s = jnp.einsum('bghqd,bgkd->bghqk', q_c, k_b) * scale
        p = masked_softmax(s, mask)
        return jnp.einsum('bghqk,bgkd->bghqd', p.astype(v_b.dtype), v_b)

    o_win = lax.map(win_block, (qw_chunks, jnp.arange(n_wb)))
    o_win = jnp.moveaxis(o_win, 0, 3).reshape(bsz, G, HPG, seq, DH)

    g = jax.nn.sigmoid(gates).reshape(bsz, seq, G, HPG, 3).transpose(0, 2, 3, 1, 4)
    o = g[..., 0:1] * o_cmp + g[..., 1:2] * o_slc + g[..., 2:3] * o_win
    return o.transpose(0, 3, 1, 2, 4).reshape(bsz, seq, NSA_WIDTH)


def gdn_mixer(qkv, z, beta_raw, a_raw, conv_w, a_log, dt_bias, norm_g):
    bsz, seq = qkv.shape[0], qkv.shape[1]
    H, DK, DV, C = GDN_HEADS, GDN_DK, GDN_DV, GDN_CHUNK
    f32 = jnp.float32
    out_dtype = qkv.dtype
    qkv = jax.nn.silu(causal_depthwise_conv(qkv, conv_w)).astype(f32)
    q, k, v = jnp.split(qkv, [H * DK, 2 * H * DK], axis=-1)
    q = q.reshape(bsz, seq, H, DK)
    k = k.reshape(bsz, seq, H, DK)
    v = v.reshape(bsz, seq, H, DV)
    q = q * lax.rsqrt(jnp.sum(q * q, -1, keepdims=True) + EPS) * (DK ** -0.5)
    k = k * lax.rsqrt(jnp.sum(k * k, -1, keepdims=True) + EPS)
    beta = jax.nn.sigmoid(beta_raw.astype(f32))
    g = -jnp.exp(a_log.astype(f32)) * jax.nn.softplus(a_raw.astype(f32) + dt_bias.astype(f32))
    n = seq // C
    q = q.reshape(bsz, n, C, H, DK).transpose(0, 3, 1, 2, 4)
    k = k.reshape(bsz, n, C, H, DK).transpose(0, 3, 1, 2, 4)
    v = v.reshape(bsz, n, C, H, DV).transpose(0, 3, 1, 2, 4)
    beta = beta.reshape(bsz, n, C, H).transpose(0, 3, 1, 2)
    gc = jnp.cumsum(g.reshape(bsz, n, C, H).transpose(0, 3, 1, 2), axis=-1)
    ii = jnp.arange(C)
    incl = ii[:, None] >= ii[None, :]
    strict = ii[:, None] > ii[None, :]
    decay = jnp.exp(jnp.where(incl, gc[..., :, None] - gc[..., None, :], -jnp.inf))
    kk = jnp.einsum('bhncd,bhnkd->bhnck', k, k)
    a_mat = jnp.where(strict, beta[..., None] * kk * decay, 0.0) + jnp.eye(C, dtype=f32)
    rhs = jnp.concatenate([k * (beta * jnp.exp(gc))[..., None], v * beta[..., None]], axis=-1)
    sol = lax.linalg.triangular_solve(a_mat, rhs, left_side=True, lower=True, unit_diagonal=True)
    w, u = sol[..., :DK], sol[..., DK:]
    qk = jnp.where(incl, jnp.einsum('bhncd,bhnkd->bhnck', q, k) * decay, 0.0)
    q_dec = q * jnp.exp(gc)[..., None]
    k_dec = k * jnp.exp(gc[..., -1:] - gc)[..., None]
    g_last = jnp.exp(gc[..., -1])
    xs = tuple(jnp.moveaxis(a, 2, 0) for a in (w, u, q_dec, k_dec, qk, g_last))

    def step(state, inp):
        w_n, u_n, qd_n, kd_n, qk_n, gl_n = inp
        v_new = u_n - jnp.einsum('bhcd,bhde->bhce', w_n, state)
        o_n = jnp.einsum('bhcd,bhde->bhce', qd_n, state) + jnp.einsum('bhck,bhke->bhce', qk_n, v_new)
        state = gl_n[..., None, None] * state + jnp.einsum('bhcd,bhce->bhde', kd_n, v_new)
        return state, o_n

    s0 = jnp.zeros((bsz, H, DK, DV), f32)
    _, o = lax.scan(step, s0, xs)
    o = o.transpose(1, 0, 3, 2, 4).reshape(bsz, seq, H, DV)
    o = rms_norm(o, norm_g) * jax.nn.silu(z.astype(f32).reshape(bsz, seq, H, DV))
    return o.reshape(bsz, seq, GDN_WIDTH).astype(out_dtype)


def setup_inputs(seed: int = 0) -> dict:
    key = jax.random.key(seed)
    ks = jax.random.split(key, 32)
    f32 = jnp.float32

    def nrm(k, shape, scale):
        return jax.random.normal(k, shape, f32) * scale

    def gain(k, shape):
        return 1.0 + 0.02 * jax.random.normal(k, shape, f32)

    x = nrm(ks[0], (BATCH, SEQ, D_MODEL), 1.0)
    c = nrm(ks[1], (BATCH, D_MODEL), 1.0)
    offs = jax.random.randint(ks[2], (BATCH, 1), 0, 4096, dtype=jnp.int32)
    positions = jnp.arange(SEQ, dtype=jnp.int32)[None, :] + offs
    dt = jnp.exp(jax.random.uniform(ks[17], (DEPTH, GDN_HEADS), f32, math.log(1e-3), math.log(1e-1)))
    return {
        "x": x,
        "c": c,
        "positions": positions,
        "ada_w": nrm(ks[3], (DEPTH, D_MODEL, N_MOD * D_MODEL), D_MODEL ** -0.5),
        "ada_b": nrm(ks[4], (DEPTH, N_MOD * D_MODEL), 0.01),
        "norm_mix_g": gain(ks[5], (DEPTH, D_MODEL)),
        "norm_ffn_g": gain(ks[6], (DEPTH, D_MODEL)),
        "w_in": nrm(ks[7], (DEPTH, D_MODEL, N_IN), D_MODEL ** -0.5),
        "b_in": nrm(ks[8], (DEPTH, N_IN), 0.01),
        "conv_dw_w": nrm(ks[9], (DEPTH, CONV_K, C_CONV), CONV_K ** -0.5),
        "conv_dw_b": nrm(ks[10], (DEPTH, C_CONV), 0.01),
        "conv_ln_g": gain(ks[11], (DEPTH, C_CONV)),
        "conv_ln_b": nrm(ks[12], (DEPTH, C_CONV), 0.01),
        "nsa_pe_k": nrm(ks[13], (DEPTH, L_CMP, NSA_HEAD_DIM), 0.02),
        "nsa_pe_v": nrm(ks[14], (DEPTH, L_CMP, NSA_HEAD_DIM), 0.02),
        "nsa_wk_cmp": nrm(ks[15], (DEPTH, L_CMP, NSA_HEAD_DIM, NSA_HEAD_DIM), (L_CMP * NSA_HEAD_DIM) ** -0.5),
        "nsa_wv_cmp": nrm(ks[16], (DEPTH, L_CMP, NSA_HEAD_DIM, NSA_HEAD_DIM), (L_CMP * NSA_HEAD_DIM) ** -0.5),
        "gdn_conv_w": nrm(ks[18], (DEPTH, GDN_CONV_K, GDN_QKV), GDN_CONV_K ** -0.5),
        "gdn_a_log": jnp.log(jax.random.uniform(ks[19], (DEPTH, GDN_HEADS), f32, 1.0, 16.0)),
        "gdn_dt_bias": dt + jnp.log(-jnp.expm1(-dt)),
        "gdn_norm_g": gain(ks[20], (DEPTH, GDN_DV)),
        "w_up_conv": nrm(ks[21], (DEPTH, C_CONV, D_MODEL), C_CONV ** -0.5),
        "w_up_nsa": nrm(ks[22], (DEPTH, NSA_WIDTH, D_MODEL), NSA_WIDTH ** -0.5),
        "w_up_gdn": nrm(ks[23], (DEPTH, GDN_WIDTH, D_MODEL), GDN_WIDTH ** -0.5),
        "w_o": nrm(ks[24], (DEPTH, D_MODEL, D_MODEL), D_MODEL ** -0.5),
        "ffn_w_in": nrm(ks[25], (DEPTH, D_MODEL, 2 * FFN_HIDDEN), D_MODEL ** -0.5),
        "ffn_w_out": nrm(ks[26], (DEPTH, FFN_HIDDEN, D_MODEL), FFN_HIDDEN ** -0.5),
        "final_norm_g": gain(ks[27], (D_MODEL,)),
    }


def reference(x, c, positions, ada_w, ada_b, norm_mix_g, norm_ffn_g, w_in, b_in,
              conv_dw_w, conv_dw_b, conv_ln_g, conv_ln_b,
              nsa_pe_k, nsa_pe_v, nsa_wk_cmp, nsa_wv_cmp,
              gdn_conv_w, gdn_a_log, gdn_dt_bias, gdn_norm_g,
              w_up_conv, w_up_nsa, w_up_gdn, w_o, ffn_w_in, ffn_w_out, final_norm_g):
    bsz, seq = x.shape[0], x.shape[1]
    c_act = jax.nn.silu(c)
    for l in range(DEPTH):
        mod = jnp.einsum('bd,de->be', c_act, ada_w[l]) + ada_b[l]
        sh1, sc1, gt1, sh2, sc2, gt2 = jnp.split(mod, N_MOD, axis=-1)

        h = rms_norm(x, norm_mix_g[l]) * (1.0 + sc1[:, None]) + sh1[:, None]
        proj = jnp.einsum('bsd,de->bse', h, w_in[l]) + b_in[l]
        (u_conv, q_nsa, kv_nsa, g_nsa, qkv_gdn, z_gdn, b_gdn, a_gdn, g_mrg) = jnp.split(proj, IN_OFFSETS, axis=-1)
        y_conv = conformer_conv(u_conv, conv_dw_w[l], conv_dw_b[l], conv_ln_g[l], conv_ln_b[l])
        y_nsa = nsa_mixer(q_nsa.reshape(bsz, seq, NSA_HEADS, NSA_HEAD_DIM), kv_nsa, g_nsa, positions,
                          nsa_pe_k[l], nsa_pe_v[l], nsa_wk_cmp[l], nsa_wv_cmp[l])
        y_gdn = gdn_mixer(qkv_gdn, z_gdn, b_gdn, a_gdn, gdn_conv_w[l], gdn_a_log[l], gdn_dt_bias[l], gdn_norm_g[l])
        gm = jax.nn.sigmoid(g_mrg).reshape(bsz, seq, 3, D_MODEL)
        merged = (gm[:, :, 0] * jnp.einsum('bsc,cd->bsd', y_conv, w_up_conv[l])
                  + gm[:, :, 1] * jnp.einsum('bsc,cd->bsd', y_nsa, w_up_nsa[l])
                  + gm[:, :, 2] * jnp.einsum('bsc,cd->bsd', y_gdn, w_up_gdn[l]))
        x = x + gt1[:, None] * jnp.einsum('bsd,de->bse', merged, w_o[l])

        h = rms_norm(x, norm_ffn_g[l]) * (1.0 + sc2[:, None]) + sh2[:, None]
        gate, up = jnp.split(jnp.einsum('bsd,df->bsf', h, ffn_w_in[l]), 2, axis=-1)
        x = x + gt2[:, None] * jnp.einsum('bsf,fd->bsd', jax.nn.silu(gate) * up, ffn_w_out[l])
    return rms_norm(x, final_norm_g)
```

```python
import functools

import numpy as np
import jax
import jax.numpy as jnp
from jax import lax
from jax.experimental import pallas as pl
from jax.experimental.pallas import tpu as pltpu

F32 = jnp.float32
BF16 = jnp.bfloat16
HI = lax.Precision.HIGHEST
NT = (((1,), (1,)), ((), ()))
TN = (((0,), (0,)), ((), ()))

EPS = 1e-6
N_MOD = 6
C_CONV = 512
CONV_HALO = 32
NSA_HEADS = 8
NSA_G = 2
NSA_HPG = NSA_HEADS // NSA_G
NSA_DH = 64
NSA_WIDTH = NSA_HEADS * NSA_DH
ROPE_DIM = NSA_DH // 4
ROPE_THETA = 500000.0
L_CMP = 32
STRIDE_CMP = 16
L_SLC = 64
N_SLC_TOP = 8
WINDOW = 512
NSA_TQ = 128
NSA_KB = 128
GDN_H = 4
GDN_DK = 128
GDN_DV = 128
GDN_WIDTH = GDN_H * GDN_DV
GDN_QKV = GDN_H * (2 * GDN_DK + GDN_DV)
GDN_CHUNK = 64
GDN_HALO = 8
SM_GATE = 0
SM_BETA = 24
SM_A = 28
LANE = 128

VMEM_LIMIT = 56 * 1024 * 1024


def _dot(a, b, dims=None, hi=False):
    if dims is None:
        dims = (((a.ndim - 1,), (0,)), ((), ()))
    if hi:
        return lax.dot_general(a.astype(F32), b.astype(F32), dims, precision=HI,
                               preferred_element_type=F32)
    return lax.dot_general(a.astype(BF16), b.astype(BF16), dims, preferred_element_type=F32)


def _masked_softmax(s, mask):
    s = jnp.where(mask, s, -1e30)
    m = jnp.max(s, axis=-1, keepdims=True)
    p = jnp.where(mask, jnp.exp(s - m), 0.0)
    return p / jnp.maximum(jnp.sum(p, axis=-1, keepdims=True), 1e-30)


def _params(*sem):
    return pltpu.CompilerParams(dimension_semantics=sem, vmem_limit_bytes=VMEM_LIMIT)


def _mod_kernel(c_ref, w_ref, b_ref, o_ref):
    c = c_ref[...]
    ca = c * jax.nn.sigmoid(c)
    o_ref[0] = _dot(ca, w_ref[0], hi=True) + b_ref[0]


def _modulation(c, ada_w, ada_b):
    depth, d, n = ada_w.shape
    bsz = c.shape[0]
    tn = 1536
    return pl.pallas_call(
        _mod_kernel,
        grid=(depth, n // tn),
        in_specs=[pl.BlockSpec((bsz, d), lambda l, j: (0, 0)),
                  pl.BlockSpec((1, d, tn), lambda l, j: (l, 0, j)),
                  pl.BlockSpec((1, 1, tn), lambda l, j: (l, 0, j))],
        out_specs=pl.BlockSpec((1, bsz, tn), lambda l, j: (l, 0, j)),
        out_shape=jax.ShapeDtypeStruct((depth, bsz, n), F32),
        compiler_params=_params("parallel", "parallel"),
        name="adaln_mod",
    )(c, ada_w, ada_b.reshape(depth, 1, n))


def _norm_mod(x, g, sh, sc):
    y = x * lax.rsqrt(jnp.mean(x * x, axis=-1, keepdims=True) + EPS)
    return (y * g) * (1.0 + sc) + sh


def _inproj_kernel(x_ref, mod_ref, g_ref, w_ref, b_ref, o_ref, h_ref):
    @pl.when(pl.program_id(1) == 0)
    def _():
        m = mod_ref[0]
        h_ref[...] = _norm_mod(x_ref[...], g_ref[...], m[0:1], m[1:2]).astype(BF16)

    o_ref[...] = jnp.dot(h_ref[...], w_ref[...], preferred_element_type=F32) + b_ref[...]


def _in_projection(x2, mod_l, g, w, b, seq, tm=1024, tn=1536):
    t, d = x2.shape
    n = w.shape[1]
    per_b = seq // tm
    return pl.pallas_call(
        _inproj_kernel,
        grid=(t // tm, n // tn),
        in_specs=[pl.BlockSpec((tm, d), lambda i, j: (i, 0)),
                  pl.BlockSpec((1, N_MOD, d), lambda i, j: (i // per_b, 0, 0)),
                  pl.BlockSpec((1, d), lambda i, j: (0, 0)),
                  pl.BlockSpec((d, tn), lambda i, j: (0, j)),
                  pl.BlockSpec((1, tn), lambda i, j: (0, j))],
        out_specs=pl.BlockSpec((tm, tn), lambda i, j: (i, j)),
        out_shape=jax.ShapeDtypeStruct((t, n), F32),
        scratch_shapes=[pltpu.VMEM((tm, d), BF16)],
        compiler_params=_params("parallel", "arbitrary"),
        name="in_proj",
    )(x2, mod_l, g, w, b)


def _conv_kernel(u_ref, w_ref, b_ref, lg_ref, lb_ref, o_ref, buf_ref, *, ts, k_len, rc):
    c = C_CONV

    @pl.when(pl.program_id(1) == 0)
    def _():
        buf_ref[0:CONV_HALO, :] = jnp.zeros((CONV_HALO, c), F32)

    @pl.when(pl.program_id(1) > 0)
    def _():
        buf_ref[0:CONV_HALO, :] = buf_ref[ts:ts + CONV_HALO, :]

    u = u_ref[...]
    buf_ref[CONV_HALO:CONV_HALO + ts, :] = u[:, :c] * jax.nn.sigmoid(u[:, c:])
    base = CONV_HALO - (k_len - 1)

    def chunk(ci, carry):
        r0 = pl.multiple_of(ci * rc, rc)
        win = buf_ref[pl.ds(r0, rc + CONV_HALO), :]
        acc = jnp.zeros((rc, c), F32)
        for k in range(k_len):
            acc = acc + w_ref[k:k + 1, :] * win[base + k:base + k + rc, :]
        y = acc + b_ref[...]
        mu = jnp.mean(y, axis=-1, keepdims=True)
        var = jnp.mean(jnp.square(y - mu), axis=-1, keepdims=True)
        y = (y - mu) * lax.rsqrt(var + EPS) * lg_ref[...] + lb_ref[...]
        o_ref[pl.ds(r0, rc), :] = (y * jax.nn.sigmoid(y)).astype(o_ref.dtype)
        return carry

    lax.fori_loop(0, ts // rc, chunk, 0)


def _conformer_conv(proj, col_blk, bsz, seq, dw_w, dw_b, ln_g, ln_b, ts=512, rc=32):
    t = bsz * seq
    k_len = dw_w.shape[0]
    per_b = seq // ts
    kern = functools.partial(_conv_kernel, ts=ts, k_len=k_len, rc=rc)
    vec = lambda: pl.BlockSpec((1, C_CONV), lambda b, s: (0, 0))
    return pl.pallas_call(
        kern,
        grid=(bsz, per_b),
        in_specs=[pl.BlockSpec((ts, 2 * C_CONV), lambda b, s: (b * per_b + s, col_blk)),
                  pl.BlockSpec((k_len, C_CONV), lambda b, s: (0, 0)),
                  vec(), vec(), vec()],
        out_specs=pl.BlockSpec((ts, C_CONV), lambda b, s: (b * per_b + s, 0)),
        out_shape=jax.ShapeDtypeStruct((t, C_CONV), BF16),
        scratch_shapes=[pltpu.VMEM((CONV_HALO + ts, C_CONV), F32)],
        compiler_params=_params("parallel", "arbitrary"),
        name="conformer_conv",
    )(proj, dw_w, dw_b.reshape(1, -1), ln_g.reshape(1, -1), ln_b.reshape(1, -1))


def _rope_table_kernel(pos_ref, inv_ref, sign_ref, cos_ref, sin_ref):
    ang = pos_ref[0].astype(F32) * inv_ref[...]
    cos_ref[0] = jnp.cos(ang)
    sin_ref[0] = jnp.sin(ang) * sign_ref[...]


def _rope_tables(positions):
    bsz, seq = positions.shape
    half = ROPE_DIM // 2
    inv = ROPE_THETA ** (-jnp.arange(half, dtype=F32) * 2.0 / ROPE_DIM)
    d = np.arange(LANE) % NSA_DH
    inv_lane = jnp.where(jnp.asarray(d < ROPE_DIM), inv[d % half], 0.0).reshape(1, LANE)
    sign_lane = jnp.asarray(np.where(d < half, -1.0, np.where(d < ROPE_DIM, 1.0, 0.0)),
                            dtype=F32).reshape(1, LANE)
    out = jax.ShapeDtypeStruct((bsz, seq, LANE), F32)
    return pl.pallas_call(
        _rope_table_kernel,
        grid=(bsz,),
        in_specs=[pl.BlockSpec((1, seq, 1), lambda b: (b, 0, 0)),
                  pl.BlockSpec((1, LANE), lambda b: (0, 0)),
                  pl.BlockSpec((1, LANE), lambda b: (0, 0))],
        out_specs=[pl.BlockSpec((1, seq, LANE), lambda b: (b, 0, 0))] * 2,
        out_shape=[out, out],
        compiler_params=_params("parallel"),
        name="rope_tables",
    )(positions.reshape(bsz, seq, 1), inv_lane, sign_lane)


def _rope_apply(x, cos_t, sin_t, width):
    lane = lax.broadcasted_iota(jnp.int32, (1, width), 1)
    lo = (lane % NSA_DH) < (ROPE_DIM // 2)
    partner = jnp.where(lo, pltpu.roll(x, width - ROPE_DIM // 2, 1), pltpu.roll(x, ROPE_DIM // 2, 1))
    return x * cos_t + partner * sin_t


def _nsa_kernel(q_ref, kv_ref, sm_ref, cos_ref, sin_ref, pe_ref, wc_ref, ovl_ref, eexp_ref, gexp_ref,
                o_ref,
                qs_ref, csrc_ref, ccat_ref, kvc_ref, kslc_ref, vslc_ref, kwin_ref, vwin_ref, mexp_ref,
                *, seq):
    tq, kb, dh, hpg = NSA_TQ, NSA_KB, NSA_DH, NSA_HPG
    nc = seq // STRIDE_CMP
    ns = seq // L_SLC
    band = WINDOW + tq
    scale = dh ** -0.5
    qw = hpg * dh

    cos_t = cos_ref[0]
    sin_t = sin_ref[0]
    cq = jnp.concatenate([cos_t] * (qw // LANE), axis=1)
    sq = jnp.concatenate([sin_t] * (qw // LANE), axis=1)
    qs_ref[...] = (_rope_apply(q_ref[...], cq, sq, qw) * scale).astype(BF16)

    lane = lax.broadcasted_iota(jnp.int32, (1, LANE), 1)
    ck = jnp.where(lane < dh, cos_t, 1.0)
    sk = jnp.where(lane < dh, sin_t, 0.0)
    kv_c = _rope_apply(kv_ref[:, 0:LANE], ck, sk, LANE)
    csrc_ref[0:seq, :] = kv_c
    csrc_ref[seq:seq + L_CMP, :] = jnp.zeros((L_CMP, LANE), F32)
    kv_s = _rope_apply(kv_ref[:, LANE:2 * LANE], ck, sk, LANE)
    kslc_ref[...] = kv_s[:, :dh].astype(BF16)
    vslc_ref[...] = kv_s[:, dh:].astype(BF16)
    kv_w = _rope_apply(kv_ref[:, 2 * LANE:3 * LANE], ck, sk, LANE)
    kwin_ref[0:WINDOW, :] = jnp.zeros((WINDOW, dh), BF16)
    vwin_ref[0:WINDOW, :] = jnp.zeros((WINDOW, dh), BF16)
    kwin_ref[WINDOW:WINDOW + seq, :] = kv_w[:, :dh].astype(BF16)
    vwin_ref[WINDOW:WINDOW + seq, :] = kv_w[:, dh:].astype(BF16)

    for l in range(L_CMP):
        ccat_ref[:, l * LANE:(l + 1) * LANE] = (
            csrc_ref[pl.ds(l, nc, stride=STRIDE_CMP), :] + pe_ref[l:l + 1, :])
    kvc = _dot(ccat_ref[...], wc_ref[...])
    kvc_ref[0] = kvc[:, :dh].astype(BF16)
    kvc_ref[1] = kvc[:, dh:].astype(BF16)

    n_lane = lax.broadcasted_iota(jnp.int32, (1, nc), 1)
    j_lane = lane
    r_col = lax.broadcasted_iota(jnp.int32, (tq, 1), 0)
    c_band = lax.broadcasted_iota(jnp.int32, (1, band), 1)
    k_lane = lax.broadcasted_iota(jnp.int32, (1, kb), 1)
    win_static = (c_band - WINDOW <= r_col) & (c_band > r_col)

    def stack_heads(x):
        return jnp.concatenate([x[:, h * dh:(h + 1) * dh] for h in range(hpg)], axis=0)

    def unstack_heads(x):
        return jnp.concatenate([x[h * tq:(h + 1) * tq] for h in range(hpg)], axis=1)

    def q_tile(i, carry):
        r0 = pl.multiple_of(i * tq, tq)
        q4 = stack_heads(qs_ref[pl.ds(r0, tq), :])
        t_col = r0 + r_col

        s = _dot(q4, kvc_ref[0], NT).reshape(hpg, tq, nc)
        cmask = (n_lane * STRIDE_CMP + (L_CMP - 1)) <= t_col
        p = _masked_softmax(s, cmask[None])
        o_cmp = _dot(p.reshape(hpg * tq, nc), kvc_ref[1])

        imp = _dot(jnp.sum(p, axis=0), ovl_ref[...], hi=True)
        cur = t_col // L_SLC
        causal_blk = j_lane <= cur
        forced = (j_lane == 0) | (j_lane == cur) | (j_lane == cur - 1)
        impv = jnp.where(forced, 1e9, jnp.where(causal_blk, imp, -1e9))
        rank = jnp.zeros((tq, LANE), F32)
        for b in range(ns):
            col = impv[:, b:b + 1]
            beats = (col > impv) | ((col == impv) & (b < j_lane))
            rank = rank + jnp.where(beats, 1.0, 0.0)
        sel = (rank < float(min(N_SLC_TOP, ns))) & causal_blk
        mexp = _dot(jnp.where(sel, 1.0, 0.0), eexp_ref[...])
        for j in range(seq // kb):
            mexp_ref[j] = mexp[:, j * kb:(j + 1) * kb]

        def kv_step(j, st):
            m, l, acc = st
            k0 = pl.multiple_of(j * kb, kb)
            s = _dot(q4, kslc_ref[pl.ds(k0, kb), :], NT).reshape(hpg, tq, kb)
            msk = ((mexp_ref[j] > 0.5) & (k0 + k_lane <= t_col))[None]
            s = jnp.where(msk, s, -1e30)
            m_new = jnp.maximum(m, jnp.max(s, axis=-1, keepdims=True))
            alpha = jnp.exp(m - m_new)
            p = jnp.where(msk, jnp.exp(s - m_new), 0.0)
            l = alpha * l + jnp.sum(p, axis=-1, keepdims=True)
            pv = _dot(p.reshape(hpg * tq, kb), vslc_ref[pl.ds(k0, kb), :])
            return m_new, l, alpha * acc + pv.reshape(hpg, tq, dh)

        init = (jnp.full((hpg, tq, 1), -1e30, F32), jnp.zeros((hpg, tq, 1), F32),
                jnp.zeros((hpg, tq, dh), F32))
        _, l_s, acc_s = lax.fori_loop(0, (r0 + tq) // kb, kv_step, init)
        o_slc = (acc_s / jnp.maximum(l_s, 1e-30)).reshape(hpg * tq, dh)

        s = _dot(q4, kwin_ref[pl.ds(r0, band), :], NT).reshape(hpg, tq, band)
        wmask = win_static & (c_band >= WINDOW - r0)
        p = _masked_softmax(s, wmask[None])
        o_win = _dot(p.reshape(hpg * tq, band), vwin_ref[pl.ds(r0, band), :])

        gate = _dot(jax.nn.sigmoid(sm_ref[pl.ds(r0, tq), :]), gexp_ref[0], hi=True)
        out = (gate[:, 0:qw] * unstack_heads(o_cmp) + gate[:, qw:2 * qw] * unstack_heads(o_slc)
               + gate[:, 2 * qw:3 * qw] * unstack_heads(o_win))
        o_ref[pl.ds(r0, tq), :] = out.astype(o_ref.dtype)
        return carry

    lax.fori_loop(0, seq // tq, q_tile, 0)


def _nsa_constants(seq):
    nc = seq // STRIDE_CMP
    ns = seq // L_SLC
    cs = np.arange(nc) * STRIDE_CMP
    js = np.arange(LANE) * L_SLC
    ovl = ((cs[:, None] < js[None, :] + L_SLC) & (cs[:, None] + L_CMP > js[None, :])
           & (np.arange(LANE)[None, :] < ns)).astype(np.float32)
    eexp = (np.arange(seq)[None, :] // L_SLC == np.arange(LANE)[:, None]).astype(np.float32)
    qw = NSA_HPG * NSA_DH
    gexp = np.zeros((NSA_G, LANE, 3 * qw), np.float32)
    for g in range(NSA_G):
        for h in range(NSA_HPG):
            for j in range(3):
                gexp[g, SM_GATE + g * NSA_HPG * 3 + h * 3 + j, j * qw + h * NSA_DH:j * qw + (h + 1) * NSA_DH] = 1.0
    return jnp.asarray(ovl), jnp.asarray(eexp, dtype=BF16), jnp.asarray(gexp)


def _nsa(proj, q_blk, kv_blk, sm_blk, bsz, seq, cos_t, sin_t, pe_k, pe_v, wk_cmp, wv_cmp, consts):
    t = bsz * seq
    dh = NSA_DH
    qw = NSA_HPG * dh
    nc = seq // STRIDE_CMP
    ovl, eexp, gexp = consts
    pe = jnp.concatenate([pe_k, pe_v], axis=1)
    zero = jnp.zeros((L_CMP, dh, dh), F32)
    wc = jnp.concatenate([jnp.concatenate([wk_cmp, zero], axis=2),
                          jnp.concatenate([zero, wv_cmp], axis=2)], axis=1)
    wc = wc.reshape(L_CMP * LANE, LANE).astype(BF16)
    kern = functools.partial(_nsa_kernel, seq=seq)
    full = lambda shape: pl.BlockSpec(shape, lambda b, g: tuple(0 for _ in shape))
    return pl.pallas_call(
        kern,
        grid=(bsz, NSA_G),
        in_specs=[pl.BlockSpec((seq, qw), lambda b, g: (b, q_blk + g)),
                  pl.BlockSpec((seq, 3 * LANE), lambda b, g: (b, kv_blk + g)),
                  pl.BlockSpec((seq, LANE), lambda b, g: (b, sm_blk)),
                  pl.BlockSpec((1, seq, LANE), lambda b, g: (b, 0, 0)),
                  pl.BlockSpec((1, seq, LANE), lambda b, g: (b, 0, 0)),
                  full((L_CMP, LANE)),
                  full((L_CMP * LANE, LANE)),
                  full((nc, LANE)),
                  full((LANE, seq)),
                  pl.BlockSpec((1, LANE, 3 * qw), lambda b, g: (g, 0, 0))],
        out_specs=pl.BlockSpec((seq, qw), lambda b, g: (b, g)),
        out_shape=jax.ShapeDtypeStruct((t, NSA_G * qw), BF16),
        scratch_shapes=[pltpu.VMEM((seq, qw), BF16),
                        pltpu.VMEM((seq + L_CMP, LANE), F32),
                        pltpu.VMEM((nc, L_CMP * LANE), F32),
                        pltpu.VMEM((2, nc, dh), BF16),
                        pltpu.VMEM((seq, dh), BF16),
                        pltpu.VMEM((seq, dh), BF16),
                        pltpu.VMEM((WINDOW + seq, dh), BF16),
                        pltpu.VMEM((WINDOW + seq, dh), BF16),
                        pltpu.VMEM((seq // NSA_KB, NSA_TQ, NSA_KB), F32)],
        compiler_params=_params("parallel", "parallel"),
        name="nsa",
    )(proj, proj, proj, cos_t, sin_t, pe, wc, ovl, eexp, gexp)


def _gdn_kernel(qkv_ref, z_ref, sm_ref, cw_ref, alog_ref, dtb_ref, ng_ref, o_ref, buf_ref, st_ref,
                *, k_len):
    c = GDN_CHUNK
    dk, dv = GDN_DK, GDN_DV

    @pl.when(pl.program_id(1) == 0)
    def _():
        buf_ref[0:GDN_HALO, :] = jnp.zeros((GDN_HALO, GDN_QKV), F32)
        st_ref[...] = jnp.zeros(st_ref.shape, F32)

    x = qkv_ref[...]
    buf_ref[GDN_HALO:GDN_HALO + c, :] = x
    base = GDN_HALO - (k_len - 1)
    acc = jnp.zeros((c, GDN_QKV), F32)
    for k in range(k_len):
        acc = acc + cw_ref[k:k + 1, :] * buf_ref[base + k:base + k + c, :]
    buf_ref[0:GDN_HALO, :] = x[c - GDN_HALO:c, :]
    xc = acc * jax.nn.sigmoid(acc)

    sm = sm_ref[...]
    beta_all = jax.nn.sigmoid(sm)
    g_all = -jnp.exp(alog_ref[...]) * jax.nn.softplus(sm + dtb_ref[...])
    ii = lax.broadcasted_iota(jnp.int32, (c, c), 0)
    jj = lax.broadcasted_iota(jnp.int32, (c, c), 1)
    incl = ii >= jj
    strict = ii > jj
    gc_all = _dot(jnp.where(incl, 1.0, 0.0), g_all, hi=True)
    ones = jnp.ones((c, c), F32)
    z = z_ref[...]

    for h in range(GDN_H):
        gcol = gc_all[:, SM_A + h:SM_A + h + 1]
        bcol = beta_all[:, SM_BETA + h:SM_BETA + h + 1]
        grow = _dot(ones, jnp.where(ii == jj, gcol, 0.0), hi=True)
        decay = jnp.exp(jnp.where(incl, gcol - grow, -jnp.inf))
        q = xc[:, h * dk:(h + 1) * dk]
        k = xc[:, GDN_H * dk + h * dk:GDN_H * dk + (h + 1) * dk]
        v = xc[:, 2 * GDN_H * dk + h * dv:2 * GDN_H * dk + (h + 1) * dv]
        q = q * lax.rsqrt(jnp.sum(q * q, axis=-1, keepdims=True) + EPS) * (dk ** -0.5)
        k = k * lax.rsqrt(jnp.sum(k * k, axis=-1, keepdims=True) + EPS)
        eg = jnp.exp(gcol)
        lmat = jnp.where(strict, bcol * _dot(k, k, NT, hi=True) * decay, 0.0)
        x_sol = jnp.concatenate([k * (bcol * eg), v * bcol], axis=1)
        pows = [lmat]
        for _ in range(5):
            pows.append(_dot(pows[-1], pows[-1], hi=True))
        for pw in pows[:0:-1]:
            x_sol = x_sol + _dot(pw, x_sol, hi=True)
        x_sol = x_sol - _dot(lmat, x_sol, hi=True)
        w = x_sol[:, :dk]
        u = x_sol[:, dk:]
        qk = jnp.where(incl, _dot(q, k, NT, hi=True) * decay, 0.0)
        glast = gcol[c - 1:c, :]
        state = st_ref[h]
        v_new = u - _dot(w, state, hi=True)
        o = _dot(q * eg, state, hi=True) + _dot(qk, v_new, hi=True)
        st_ref[h] = jnp.exp(glast) * state + _dot(k * jnp.exp(glast - gcol), v_new, TN, hi=True)
        o = o * lax.rsqrt(jnp.mean(o * o, axis=-1, keepdims=True) + EPS) * ng_ref[...]
        zh = z[:, h * dv:(h + 1) * dv]
        o_ref[:, h * dv:(h + 1) * dv] = (o * (zh * jax.nn.sigmoid(zh))).astype(o_ref.dtype)


def _gdn(proj, qkv_blk, z_blk, sm_blk, bsz, seq, conv_w, a_log, dt_bias, norm_g):
    t = bsz * seq
    c = GDN_CHUNK
    nch = seq // c
    k_len = conv_w.shape[0]
    alog = jnp.zeros((1, LANE), F32).at[0, SM_A:SM_A + GDN_H].set(a_log)
    dtb = jnp.zeros((1, LANE), F32).at[0, SM_A:SM_A + GDN_H].set(dt_bias)
    kern = functools.partial(_gdn_kernel, k_len=k_len)
    full = lambda shape: pl.BlockSpec(shape, lambda b, n: tuple(0 for _ in shape))
    return pl.pallas_call(
        kern,
        grid=(bsz, nch),
        in_specs=[pl.BlockSpec((c, GDN_QKV), lambda b, n: (b * nch + n, qkv_blk)),
                  pl.BlockSpec((c, GDN_WIDTH), lambda b, n: (b * nch + n, z_blk)),
                  pl.BlockSpec((c, LANE), lambda b, n: (b * nch + n, sm_blk)),
                  full((k_len, GDN_QKV)), full((1, LANE)), full((1, LANE)), full((1, GDN_DV))],
        out_specs=pl.BlockSpec((c, GDN_WIDTH), lambda b, n: (b * nch + n, 0)),
        out_shape=jax.ShapeDtypeStruct((t, GDN_WIDTH), BF16),
        scratch_shapes=[pltpu.VMEM((GDN_HALO + c, GDN_QKV), F32),
                        pltpu.VMEM((GDN_H, GDN_DK, GDN_DV), F32)],
        compiler_params=_params("parallel", "arbitrary"),
        name="gdn",
    )(proj, proj, proj, conv_w, alog, dtb, norm_g.reshape(1, -1))


def _merge_kernel(x_ref, mod_ref, yc_ref, yn_ref, yg_ref, gm_ref, wc_ref, wn_ref, wg_ref, wo_ref, o_ref):
    d = x_ref.shape[1]
    gm = jax.nn.sigmoid(gm_ref[...])
    merged = (gm[:, 0:d] * jnp.dot(yc_ref[...], wc_ref[...], preferred_element_type=F32)
              + gm[:, d:2 * d] * jnp.dot(yn_ref[...], wn_ref[...], preferred_element_type=F32)
              + gm[:, 2 * d:3 * d] * jnp.dot(yg_ref[...], wg_ref[...], preferred_element_type=F32))
    gt = mod_ref[0][2:3]
    o_ref[...] = x_ref[...] + gt * _dot(merged, wo_ref[...])


def _merge(x2, mod_l, y_conv, y_nsa, y_gdn, proj, gm_blk, w_c, w_n, w_g, w_o, seq, tm=512):
    t, d = x2.shape
    per_b = seq // tm
    row = lambda w: pl.BlockSpec((tm, w), lambda i: (i, 0))
    wspec = lambda w: pl.BlockSpec(w.shape, lambda i: (0, 0))
    return pl.pallas_call(
        _merge_kernel,
        grid=(t // tm,),
        in_specs=[row(d),
                  pl.BlockSpec((1, N_MOD, d), lambda i: (i // per_b, 0, 0)),
                  row(y_conv.shape[1]), row(y_nsa.shape[1]), row(y_gdn.shape[1]),
                  pl.BlockSpec((tm, 3 * d), lambda i: (i, gm_blk)),
                  wspec(w_c), wspec(w_n), wspec(w_g), wspec(w_o)],
        out_specs=row(d),
        out_shape=jax.ShapeDtypeStruct((t, d), F32),
        compiler_params=_params("parallel"),
        name="merge",
    )(x2, mod_l, y_conv, y_nsa, y_gdn, proj, w_c, w_n, w_g, w_o)


def _ffn_kernel(x_ref, mod_ref, g_ref, wg_ref, wu_ref, wo_ref, fg_ref, o_ref, h_ref, acc_ref, *, final):
    j = pl.program_id(1)

    @pl.when(j == 0)
    def _():
        m = mod_ref[0]
        h_ref[...] = _norm_mod(x_ref[...], g_ref[...], m[3:4], m[4:5]).astype(BF16)
        acc_ref[...] = jnp.zeros(acc_ref.shape, F32)

    h = h_ref[...]
    gate = jnp.dot(h, wg_ref[...], preferred_element_type=F32)
    up = jnp.dot(h, wu_ref[...], preferred_element_type=F32)
    act = (gate * jax.nn.sigmoid(gate)) * up
    acc_ref[...] += _dot(act, wo_ref[...])

    @pl.when(j == pl.num_programs(1) - 1)
    def _():
        y = x_ref[...] + mod_ref[0][5:6] * acc_ref[...]
        if final:
            y = y * lax.rsqrt(jnp.mean(y * y, axis=-1, keepdims=True) + EPS) * fg_ref[...]
        o_ref[...] = y


def _ffn(x2, mod_l, g, w_in, w_out, final_g, seq, final, tm=512, th=1408):
    t, d = x2.shape
    hid = w_out.shape[0]
    per_b = seq // tm
    nh = hid // th
    kern = functools.partial(_ffn_kernel, final=final)
    return pl.pallas_call(
        kern,
        grid=(t // tm, nh),
        in_specs=[pl.BlockSpec((tm, d), lambda i, j: (i, 0)),
                  pl.BlockSpec((1, N_MOD, d), lambda i, j: (i // per_b, 0, 0)),
                  pl.BlockSpec((1, d), lambda i, j: (0, 0)),
                  pl.BlockSpec((d, th), lambda i, j: (0, j)),
                  pl.BlockSpec((d, th), lambda i, j: (0, nh + j)),
                  pl.BlockSpec((th, d), lambda i, j: (j, 0)),
                  pl.BlockSpec((1, d), lambda i, j: (0, 0))],
        out_specs=pl.BlockSpec((tm, d), lambda i, j: (i, 0)),
        out_shape=jax.ShapeDtypeStruct((t, d), F32),
        scratch_shapes=[pltpu.VMEM((tm, d), BF16), pltpu.VMEM((tm, d), F32)],
        compiler_params=_params("parallel", "arbitrary"),
        name="ffn",
    )(x2, mod_l, g, w_in, w_in, w_out, final_g)


def _in_layout(d):
    segs = {}
    off = 0
    for name, width in (("gm", 3 * d), ("conv", 2 * C_CONV), ("q", NSA_WIDTH), ("qkv", GDN_QKV),
                        ("z", GDN_WIDTH), ("small", LANE), ("pad", LANE),
                        ("kv", 6 * NSA_G * NSA_DH)):
        segs[name] = off
        off += width
    return segs, off


def _rearrange_in_proj(w_in, b_in, d):
    splits = (2 * C_CONV, NSA_WIDTH, 6 * NSA_G * NSA_DH, 3 * NSA_HEADS, GDN_QKV, GDN_WIDTH, GDN_H, GDN_H, 3 * d)
    offs = np.concatenate([[0], np.cumsum(splits)])
    src = {n: np.arange(offs[i], offs[i + 1]) for i, n in enumerate(
        ("conv", "q", "kv", "gnsa", "qkv", "z", "beta", "a", "gm"))}
    kv = src["kv"].reshape(6, NSA_G, NSA_DH).transpose(1, 0, 2).reshape(-1)
    segs, total = _in_layout(d)
    col = np.full((total,), -1, np.int64)
    col[segs["gm"]:segs["gm"] + 3 * d] = src["gm"]
    col[segs["conv"]:segs["conv"] + 2 * C_CONV] = src["conv"]
    col[segs["q"]:segs["q"] + NSA_WIDTH] = src["q"]
    col[segs["qkv"]:segs["qkv"] + GDN_QKV] = src["qkv"]
    col[segs["z"]:segs["z"] + GDN_WIDTH] = src["z"]
    sm = segs["small"]
    col[sm + SM_GATE:sm + SM_GATE + 3 * NSA_HEADS] = src["gnsa"]
    col[sm + SM_BETA:sm + SM_BETA + GDN_H] = src["beta"]
    col[sm + SM_A:sm + SM_A + GDN_H] = src["a"]
    col[segs["kv"]:segs["kv"] + kv.size] = kv
    valid = col >= 0
    idx = jnp.asarray(np.where(valid, col, 0))
    keep = jnp.asarray(valid)
    w = jnp.where(keep[None, None, :], jnp.take(w_in, idx, axis=2), 0.0).astype(BF16)
    b = jnp.where(keep[None, :], jnp.take(b_in, idx, axis=1), 0.0)
    return w, b[:, None, :], segs


def kernel(x, c, positions, ada_w, ada_b, norm_mix_g, norm_ffn_g, w_in, b_in, conv_dw_w, conv_dw_b, conv_ln_g, conv_ln_b, nsa_pe_k, nsa_pe_v, nsa_wk_cmp, nsa_wv_cmp, gdn_conv_w, gdn_a_log, gdn_dt_bias, gdn_norm_g, w_up_conv, w_up_nsa, w_up_gdn, w_o, ffn_w_in, ffn_w_out, final_norm_g):
    bsz, seq, d = x.shape
    depth = ada_w.shape[0]
    t = bsz * seq

    mod = _modulation(c, ada_w, ada_b).reshape(depth, bsz, N_MOD, d)
    cos_t, sin_t = _rope_tables(positions)
    nsa_consts = _nsa_constants(seq)
    w_in_r, b_in_r, segs = _rearrange_in_proj(w_in, b_in, d)
    blk = lambda name, width: segs[name] // width
    qw = NSA_HPG * NSA_DH

    x2 = x.reshape(t, d)
    for l in range(depth):
        proj = _in_projection(x2, mod[l], norm_mix_g[l].reshape(1, d), w_in_r[l], b_in_r[l], seq)
        y_conv = _conformer_conv(proj, blk("conv", 2 * C_CONV), bsz, seq,
                                 conv_dw_w[l], conv_dw_b[l], conv_ln_g[l], conv_ln_b[l])
        y_nsa = _nsa(proj, blk("q", qw), blk("kv", 3 * LANE), blk("small", LANE), bsz, seq, cos_t, sin_t,
                     nsa_pe_k[l], nsa_pe_v[l], nsa_wk_cmp[l], nsa_wv_cmp[l], nsa_consts)
        y_gdn = _gdn(proj, blk("qkv", GDN_QKV), blk("z", GDN_WIDTH), blk("small", LANE), bsz, seq,
                     gdn_conv_w[l], gdn_a_log[l], gdn_dt_bias[l], gdn_norm_g[l])
        x2 = _merge(x2, mod[l], y_conv, y_nsa, y_gdn, proj, blk("gm", 3 * d),
                    w_up_conv[l].astype(BF16), w_up_nsa[l].astype(BF16), w_up_gdn[l].astype(BF16),
                    w_o[l].astype(BF16), seq)
        x2 = _ffn(x2, mod[l], norm_ffn_g[l].reshape(1, d), ffn_w_in[l].astype(BF16),
                  ffn_w_out[l].astype(BF16), final_norm_g.reshape(1, d), seq, final=(l == depth - 1))
    return x2.reshape(bsz, seq, d)
```

```python
import functools

import numpy as np
import jax
import jax.numpy as jnp
from jax import lax
from jax.experimental import pallas as pl
from jax.experimental.pallas import tpu as pltpu

F32 = jnp.float32
BF16 = jnp.bfloat16
HI = lax.Precision.HIGHEST
NT = (((1,), (1,)), ((), ()))
TN = (((0,), (0,)), ((), ()))

EPS = 1e-6
N_MOD = 6
C_CONV = 512
CONV_HALO = 32
NSA_HEADS = 8
NSA_G = 2
NSA_HPG = NSA_HEADS // NSA_G
NSA_DH = 64
NSA_WIDTH = NSA_HEADS * NSA_DH
ROPE_DIM = NSA_DH // 4
ROPE_THETA = 500000.0
L_CMP = 32
STRIDE_CMP = 16
L_SLC = 64
N_SLC_TOP = 8
WINDOW = 512
NSA_TQ = 128
NSA_KB = 256
GDN_H = 4
GDN_DK = 128
GDN_DV = 128
GDN_WIDTH = GDN_H * GDN_DV
GDN_QKV = GDN_H * (2 * GDN_DK + GDN_DV)
GDN_CHUNK = 64
GDN_HALO = 8
SM_GATE = 0
SM_BETA = 24
SM_A = 28
LANE = 128

VMEM_LIMIT = 56 * 1024 * 1024


def _dot(a, b, dims=None, hi=False):
    if dims is None:
        dims = (((a.ndim - 1,), (0,)), ((), ()))
    if hi:
        return lax.dot_general(a.astype(F32), b.astype(F32), dims, precision=HI,
                               preferred_element_type=F32)
    return lax.dot_general(a.astype(BF16), b.astype(BF16), dims, preferred_element_type=F32)


def _masked_softmax0(s, mask):
    s = jnp.where(mask, s, -1e30)
    m = jnp.max(s, axis=0, keepdims=True)
    p = jnp.where(mask, jnp.exp(s - m), 0.0)
    return p * (1.0 / jnp.maximum(jnp.sum(p, axis=0, keepdims=True), 1e-30))


def _params(*sem):
    return pltpu.CompilerParams(dimension_semantics=sem, vmem_limit_bytes=VMEM_LIMIT)


def _mod_kernel(c_ref, w_ref, b_ref, o_ref):
    c = c_ref[...]
    ca = c * jax.nn.sigmoid(c)
    o_ref[0] = _dot(ca, w_ref[0], hi=True) + b_ref[0]


def _modulation(c, ada_w, ada_b):
    depth, d, n = ada_w.shape
    bsz = c.shape[0]
    tn = 1536
    return pl.pallas_call(
        _mod_kernel,
        grid=(depth, n // tn),
        in_specs=[pl.BlockSpec((bsz, d), lambda l, j: (0, 0)),
                  pl.BlockSpec((1, d, tn), lambda l, j: (l, 0, j)),
                  pl.BlockSpec((1, 1, tn), lambda l, j: (l, 0, j))],
        out_specs=pl.BlockSpec((1, bsz, tn), lambda l, j: (l, 0, j)),
        out_shape=jax.ShapeDtypeStruct((depth, bsz, n), F32),
        compiler_params=_params("parallel", "parallel"),
        name="adaln_mod",
    )(c, ada_w, ada_b.reshape(depth, 1, n))


def _norm_mod(x, g, sh, sc):
    y = x * lax.rsqrt(jnp.mean(x * x, axis=-1, keepdims=True) + EPS)
    return (y * g) * (1.0 + sc) + sh


def _inproj_kernel(x_ref, mod_ref, g_ref, w_ref, b_ref, o_ref, h_ref):
    @pl.when(pl.program_id(1) == 0)
    def _():
        m = mod_ref[0]
        h_ref[...] = _norm_mod(x_ref[...], g_ref[...], m[0:1], m[1:2]).astype(BF16)

    o_ref[...] = jnp.dot(h_ref[...], w_ref[...], preferred_element_type=F32) + b_ref[...]


def _in_projection(x2, mod_l, g, w, b, seq, tm=1024, tn=1536):
    t, d = x2.shape
    n = w.shape[1]
    per_b = seq // tm
    return pl.pallas_call(
        _inproj_kernel,
        grid=(t // tm, n // tn),
        in_specs=[pl.BlockSpec((tm, d), lambda i, j: (i, 0)),
                  pl.BlockSpec((1, N_MOD, d), lambda i, j: (i // per_b, 0, 0)),
                  pl.BlockSpec((1, d), lambda i, j: (0, 0)),
                  pl.BlockSpec((d, tn), lambda i, j: (0, j)),
                  pl.BlockSpec((1, tn), lambda i, j: (0, j))],
        out_specs=pl.BlockSpec((tm, tn), lambda i, j: (i, j)),
        out_shape=jax.ShapeDtypeStruct((t, n), F32),
        scratch_shapes=[pltpu.VMEM((tm, d), BF16)],
        compiler_params=_params("parallel", "arbitrary"),
        name="in_proj",
    )(x2, mod_l, g, w, b)


def _conv_kernel(u_ref, w_ref, b_ref, lg_ref, lb_ref, o_ref, buf_ref, *, ts, k_len, rc):
    c = C_CONV

    @pl.when(pl.program_id(1) == 0)
    def _():
        buf_ref[0:CONV_HALO, :] = jnp.zeros((CONV_HALO, c), F32)

    @pl.when(pl.program_id(1) > 0)
    def _():
        buf_ref[0:CONV_HALO, :] = buf_ref[ts:ts + CONV_HALO, :]

    u = u_ref[...]
    buf_ref[CONV_HALO:CONV_HALO + ts, :] = u[:, :c] * jax.nn.sigmoid(u[:, c:])
    base = CONV_HALO - (k_len - 1)

    def chunk(ci, carry):
        r0 = pl.multiple_of(ci * rc, rc)
        win = buf_ref[pl.ds(r0, rc + CONV_HALO), :]
        acc = jnp.zeros((rc, c), F32)
        for k in range(k_len):
            acc = acc + w_ref[k:k + 1, :] * win[base + k:base + k + rc, :]
        y = acc + b_ref[...]
        mu = jnp.mean(y, axis=-1, keepdims=True)
        var = jnp.mean(jnp.square(y - mu), axis=-1, keepdims=True)
        y = (y - mu) * lax.rsqrt(var + EPS) * lg_ref[...] + lb_ref[...]
        o_ref[pl.ds(r0, rc), :] = (y * jax.nn.sigmoid(y)).astype(o_ref.dtype)
        return carry

    lax.fori_loop(0, ts // rc, chunk, 0)


def _conformer_conv(proj, col_blk, bsz, seq, dw_w, dw_b, ln_g, ln_b, ts=512, rc=32):
    t = bsz * seq
    k_len = dw_w.shape[0]
    per_b = seq // ts
    kern = functools.partial(_conv_kernel, ts=ts, k_len=k_len, rc=rc)
    vec = lambda: pl.BlockSpec((1, C_CONV), lambda b, s: (0, 0))
    return pl.pallas_call(
        kern,
        grid=(bsz, per_b),
        in_specs=[pl.BlockSpec((ts, 2 * C_CONV), lambda b, s: (b * per_b + s, col_blk)),
                  pl.BlockSpec((k_len, C_CONV), lambda b, s: (0, 0)),
                  vec(), vec(), vec()],
        out_specs=pl.BlockSpec((ts, C_CONV), lambda b, s: (b * per_b + s, 0)),
        out_shape=jax.ShapeDtypeStruct((t, C_CONV), BF16),
        scratch_shapes=[pltpu.VMEM((CONV_HALO + ts, C_CONV), F32)],
        compiler_params=_params("parallel", "arbitrary"),
        name="conformer_conv",
    )(proj, dw_w, dw_b.reshape(1, -1), ln_g.reshape(1, -1), ln_b.reshape(1, -1))


def _rope_table_kernel(pos_ref, inv_ref, sign_ref, cos_ref, sin_ref):
    ang = pos_ref[0].astype(F32) * inv_ref[...]
    cos_ref[0] = jnp.cos(ang)
    sin_ref[0] = jnp.sin(ang) * sign_ref[...]


def _rope_tables(positions):
    bsz, seq = positions.shape
    half = ROPE_DIM // 2
    inv = ROPE_THETA ** (-jnp.arange(half, dtype=F32) * 2.0 / ROPE_DIM)
    d = np.arange(LANE) % NSA_DH
    inv_lane = jnp.where(jnp.asarray(d < ROPE_DIM), inv[d % half], 0.0).reshape(1, LANE)
    sign_lane = jnp.asarray(np.where(d < half, -1.0, np.where(d < ROPE_DIM, 1.0, 0.0)),
                            dtype=F32).reshape(1, LANE)
    out = jax.ShapeDtypeStruct((bsz, seq, LANE), F32)
    return pl.pallas_call(
        _rope_table_kernel,
        grid=(bsz,),
        in_specs=[pl.BlockSpec((1, seq, 1), lambda b: (b, 0, 0)),
                  pl.BlockSpec((1, LANE), lambda b: (0, 0)),
                  pl.BlockSpec((1, LANE), lambda b: (0, 0))],
        out_specs=[pl.BlockSpec((1, seq, LANE), lambda b: (b, 0, 0))] * 2,
        out_shape=[out, out],
        compiler_params=_params("parallel"),
        name="rope_tables",
    )(positions.reshape(bsz, seq, 1), inv_lane, sign_lane)


def _rope_apply(x, cos_t, sin_t, width):
    lane = lax.broadcasted_iota(jnp.int32, (1, width), 1)
    lo = (lane % NSA_DH) < (ROPE_DIM // 2)
    partner = jnp.where(lo, pltpu.roll(x, width - ROPE_DIM // 2, 1), pltpu.roll(x, ROPE_DIM // 2, 1))
    return x * cos_t + partner * sin_t


def _nsa_kernel(q_ref, kv_ref, sm_ref, cos_ref, sin_ref, pe_ref, wc_ref, ovl_ref, eexp_ref, gsel_ref,
                o_ref,
                qt_ref, csrc_ref, ccat_ref, kc_ref, vct_ref, kslc_ref, vslct_ref, kwin_ref, vwint_ref,
                mfull_ref, *, seq):
    tq, kb, dh, hpg = NSA_TQ, NSA_KB, NSA_DH, NSA_HPG
    nc = seq // STRIDE_CMP
    ns = seq // L_SLC
    nwb = WINDOW // tq
    band = WINDOW + tq
    scale = dh ** -0.5
    qw = hpg * dh
    n_top = float(min(N_SLC_TOP, ns))
    lane = lax.broadcasted_iota(jnp.int32, (1, LANE), 1)

    csrc_ref[seq:seq + L_CMP, :] = jnp.zeros((L_CMP, LANE), F32)
    kwin_ref[0:WINDOW, :] = jnp.zeros((WINDOW, dh), BF16)
    vwint_ref[0:nwb] = jnp.zeros((nwb, dh, tq), BF16)

    def prep(blk, carry):
        r0 = pl.multiple_of(blk * tq, tq)
        cos_t = cos_ref[0, pl.ds(r0, tq), :]
        sin_t = sin_ref[0, pl.ds(r0, tq), :]
        cq = jnp.concatenate([cos_t] * (qw // LANE), axis=1)
        sq = jnp.concatenate([sin_t] * (qw // LANE), axis=1)
        qr = _rope_apply(q_ref[pl.ds(r0, tq), :], cq, sq, qw) * scale
        qt_ref[blk] = qr.T.astype(BF16)
        ck = jnp.where(lane < dh, cos_t, 1.0)
        sk = jnp.where(lane < dh, sin_t, 0.0)
        kv = kv_ref[pl.ds(r0, tq), :]
        csrc_ref[pl.ds(r0, tq), :] = _rope_apply(kv[:, 0:LANE], ck, sk, LANE)
        kv_s = _rope_apply(kv[:, LANE:2 * LANE], ck, sk, LANE)
        kslc_ref[pl.ds(r0, tq), :] = kv_s[:, :dh].astype(BF16)
        vslct_ref[blk] = kv_s.T[dh:, :].astype(BF16)
        kv_w = _rope_apply(kv[:, 2 * LANE:3 * LANE], ck, sk, LANE)
        kwin_ref[pl.ds(pl.multiple_of(WINDOW + r0, tq), tq), :] = kv_w[:, :dh].astype(BF16)
        vwint_ref[nwb + blk] = kv_w.T[dh:, :].astype(BF16)
        return carry

    lax.fori_loop(0, seq // tq, prep, 0)

    for l in range(L_CMP):
        ccat_ref[:, l * LANE:(l + 1) * LANE] = (
            csrc_ref[pl.ds(l, nc, stride=STRIDE_CMP), :] + pe_ref[l:l + 1, :])
    kvc = _dot(ccat_ref[...], wc_ref[...])
    kc_ref[...] = kvc[:, :dh].astype(BF16)
    vct_ref[...] = kvc.T[dh:, :].astype(BF16)

    n_sub = lax.broadcasted_iota(jnp.int32, (nc, 1), 0)
    j_sub = lax.broadcasted_iota(jnp.int32, (ns, 1), 0)
    k_sub = lax.broadcasted_iota(jnp.int32, (kb, 1), 0)
    c_sub = lax.broadcasted_iota(jnp.int32, (band, 1), 0)
    q_lane = lax.broadcasted_iota(jnp.int32, (1, tq), 1)
    win_static = (c_sub - WINDOW <= q_lane) & (c_sub > q_lane)
    heads = [slice(h * tq, (h + 1) * tq) for h in range(hpg)]

    def q_tile(i, carry):
        r0 = pl.multiple_of(i * tq, tq)
        qti = qt_ref[i]
        qt4 = jnp.concatenate([qti[h * dh:(h + 1) * dh, :] for h in range(hpg)], axis=1)
        t_lane = r0 + q_lane

        s = _dot(kc_ref[...], qt4)
        cmask = (n_sub * STRIDE_CMP + (L_CMP - 1)) <= t_lane
        ps = [_masked_softmax0(s[:, sl], cmask) for sl in heads]
        o_cmp = _dot(vct_ref[...], jnp.concatenate([p.astype(BF16) for p in ps], axis=1))
        psum = ps[0]
        for p in ps[1:]:
            psum = psum + p

        imp = _dot(ovl_ref[...], psum, hi=True)
        cur = t_lane // L_SLC
        causal_blk = j_sub <= cur
        forced = (j_sub == 0) | (j_sub == cur) | (j_sub == cur - 1)
        impv = jnp.where(forced, 1e9, jnp.where(causal_blk, imp, -1e9))
        rank = jnp.zeros((ns, tq), F32)
        for b in range(ns):
            row = impv[b:b + 1, :]
            beats = (row > impv) | ((row == impv) & (b < j_sub))
            rank = rank + jnp.where(beats, 1.0, 0.0)
        sel = jnp.where((rank < n_top) & causal_blk, 1.0, 0.0)
        mfull_ref[...] = _dot(eexp_ref[...], sel)

        s = _dot(kwin_ref[pl.ds(r0, band), :], qt4)
        wmask = win_static & (c_sub >= WINDOW - r0)
        p4 = jnp.concatenate([_masked_softmax0(s[:, sl], wmask).astype(BF16) for sl in heads], axis=1)
        o_win = _dot(vwint_ref[i], p4[0:tq, :])
        for jb in range(1, nwb + 1):
            o_win = o_win + _dot(vwint_ref[i + jb], p4[jb * tq:(jb + 1) * tq, :])

        gt = _dot(gsel_ref[0], jax.nn.sigmoid(sm_ref[pl.ds(r0, tq), :]), NT, hi=True)
        gate = lambda br: jnp.concatenate(
            [jnp.broadcast_to(gt[br * hpg + h:br * hpg + h + 1], (dh, tq)) for h in range(hpg)], axis=1)
        o_cw = gate(0) * o_cmp + gate(2) * o_win
        g_slc = gate(1)

        def kv_step(j, st):
            m, l, acc, s = st
            k0 = pl.multiple_of(j * kb, kb)
            kn = pl.multiple_of(jnp.minimum(j + 1, seq // kb - 1) * kb, kb)
            s_next = _dot(kslc_ref[pl.ds(kn, kb), :], qt4)
            msk = (mfull_ref[pl.ds(k0, kb), :] > 0.5) & (k0 + k_sub <= t_lane)
            ms, ls, als, ps = [], [], [], []
            for sl in heads:
                sh = jnp.where(msk, s[:, sl], -1e30)
                mn = jnp.maximum(m[:, sl], jnp.max(sh, axis=0, keepdims=True))
                al = jnp.exp(m[:, sl] - mn)
                p = jnp.where(msk, jnp.exp(sh - mn), 0.0)
                ms.append(mn)
                als.append(al)
                ls.append(al * l[:, sl] + jnp.sum(p, axis=0, keepdims=True))
                ps.append(p.astype(BF16))
            p4 = jnp.concatenate(ps, axis=1)
            pv = _dot(vslct_ref[j * (kb // tq)], p4[0:tq, :])
            for hf in range(1, kb // tq):
                pv = pv + _dot(vslct_ref[j * (kb // tq) + hf], p4[hf * tq:(hf + 1) * tq, :])
            return (jnp.concatenate(ms, axis=1), jnp.concatenate(ls, axis=1),
                    jnp.concatenate(als, axis=1) * acc + pv, s_next)

        init = (jnp.full((1, hpg * tq), -1e30, F32), jnp.zeros((1, hpg * tq), F32),
                jnp.zeros((dh, hpg * tq), F32), _dot(kslc_ref[0:kb, :], qt4))
        _, l_s, acc_s, _ = lax.fori_loop(0, (r0 + tq + kb - 1) // kb, kv_step, init)
        o_slc = acc_s * (1.0 / jnp.maximum(l_s, 1e-30))

        out_t = o_cw + g_slc * o_slc
        out_t = jnp.concatenate([out_t[:, sl] for sl in heads], axis=0)
        o_ref[pl.ds(r0, tq), :] = out_t.T.astype(o_ref.dtype)
        return carry

    lax.fori_loop(0, seq // tq, q_tile, 0)


def _nsa_constants(seq):
    nc = seq // STRIDE_CMP
    ns = seq // L_SLC
    cs = np.arange(nc) * STRIDE_CMP
    js = np.arange(ns) * L_SLC
    ovl_t = ((cs[None, :] < js[:, None] + L_SLC) & (cs[None, :] + L_CMP > js[:, None])).astype(np.float32)
    eexp_t = (np.arange(seq)[:, None] // L_SLC == np.arange(ns)[None, :]).astype(np.float32)
    gsel = np.zeros((NSA_G, 16, LANE), np.float32)
    for g in range(NSA_G):
        for h in range(NSA_HPG):
            for j in range(3):
                gsel[g, j * NSA_HPG + h, SM_GATE + (g * NSA_HPG + h) * 3 + j] = 1.0
    return jnp.asarray(ovl_t), jnp.asarray(eexp_t, dtype=BF16), jnp.asarray(gsel)


def _nsa(proj, q_blk, kv_blk, sm_blk, bsz, seq, cos_t, sin_t, pe_k, pe_v, wk_cmp, wv_cmp, consts):
    t = bsz * seq
    dh = NSA_DH
    qw = NSA_HPG * dh
    tq = NSA_TQ
    nc = seq // STRIDE_CMP
    ns = seq // L_SLC
    ovl_t, eexp_t, gsel = consts
    pe = jnp.concatenate([pe_k, pe_v], axis=1)
    zero = jnp.zeros((L_CMP, dh, dh), F32)
    wc = jnp.concatenate([jnp.concatenate([wk_cmp, zero], axis=2),
                          jnp.concatenate([zero, wv_cmp], axis=2)], axis=1)
    wc = wc.reshape(L_CMP * LANE, LANE).astype(BF16)
    kern = functools.partial(_nsa_kernel, seq=seq)
    full = lambda shape: pl.BlockSpec(shape, lambda b, g: tuple(0 for _ in shape))
    return pl.pallas_call(
        kern,
        grid=(bsz, NSA_G),
        in_specs=[pl.BlockSpec((seq, qw), lambda b, g: (b, q_blk + g)),
                  pl.BlockSpec((seq, 3 * LANE), lambda b, g: (b, kv_blk + g)),
                  pl.BlockSpec((seq, LANE), lambda b, g: (b, sm_blk)),
                  pl.BlockSpec((1, seq, LANE), lambda b, g: (b, 0, 0)),
                  pl.BlockSpec((1, seq, LANE), lambda b, g: (b, 0, 0)),
                  full((L_CMP, LANE)),
                  full((L_CMP * LANE, LANE)),
                  full((ns, nc)),
                  full((seq, ns)),
                  pl.BlockSpec((1, 16, LANE), lambda b, g: (g, 0, 0))],
        out_specs=pl.BlockSpec((seq, qw), lambda b, g: (b, g)),
        out_shape=jax.ShapeDtypeStruct((t, NSA_G * qw), BF16),
        scratch_shapes=[pltpu.VMEM((seq // tq, qw, tq), BF16),
                        pltpu.VMEM((seq + L_CMP, LANE), F32),
                        pltpu.VMEM((nc, L_CMP * LANE), F32),
                        pltpu.VMEM((nc, dh), BF16),
                        pltpu.VMEM((dh, nc), BF16),
                        pltpu.VMEM((seq, dh), BF16),
                        pltpu.VMEM((seq // tq, dh, tq), BF16),
                        pltpu.VMEM((WINDOW + seq, dh), BF16),
                        pltpu.VMEM(((WINDOW + seq) // tq, dh, tq), BF16),
                        pltpu.VMEM((seq, tq), F32)],
        compiler_params=_params("parallel", "parallel"),
        name="nsa",
    )(proj, proj, proj, cos_t, sin_t, pe, wc, ovl_t, eexp_t, gsel)


def _gdn_kernel(qkv_ref, z_ref, sm_ref, cw_ref, alog_ref, dtb_ref, ng_ref, o_ref, buf_ref, st_ref,
                *, k_len):
    c = GDN_CHUNK
    dk, dv = GDN_DK, GDN_DV

    @pl.when(pl.program_id(1) == 0)
    def _():
        buf_ref[0:GDN_HALO, :] = jnp.zeros((GDN_HALO, GDN_QKV), F32)
        st_ref[...] = jnp.zeros(st_ref.shape, F32)

    x = qkv_ref[...]
    buf_ref[GDN_HALO:GDN_HALO + c, :] = x
    base = GDN_HALO - (k_len - 1)
    acc = jnp.zeros((c, GDN_QKV), F32)
    for k in range(k_len):
        acc = acc + cw_ref[k:k + 1, :] * buf_ref[base + k:base + k + c, :]
    buf_ref[0:GDN_HALO, :] = x[c - GDN_HALO:c, :]
    xc = acc * jax.nn.sigmoid(acc)

    sm = sm_ref[...]
    beta_all = jax.nn.sigmoid(sm)
    g_all = -jnp.exp(alog_ref[...]) * jax.nn.softplus(sm + dtb_ref[...])
    ii = lax.broadcasted_iota(jnp.int32, (c, c), 0)
    jj = lax.broadcasted_iota(jnp.int32, (c, c), 1)
    incl = ii >= jj
    strict = ii > jj
    gc_all = _dot(jnp.where(incl, 1.0, 0.0), g_all, hi=True)
    gc_rows = gc_all.T
    z = z_ref[...]
    ng = c // 8

    for h in range(GDN_H):
        gcol = gc_all[:, SM_A + h:SM_A + h + 1]
        bcol = beta_all[:, SM_BETA + h:SM_BETA + h + 1]
        grow = gc_rows[SM_A + h:SM_A + h + 1, :]
        decay = jnp.exp(jnp.where(incl, gcol - grow, -jnp.inf))
        q = xc[:, h * dk:(h + 1) * dk]
        k = xc[:, GDN_H * dk + h * dk:GDN_H * dk + (h + 1) * dk]
        v = xc[:, 2 * GDN_H * dk + h * dv:2 * GDN_H * dk + (h + 1) * dv]
        q = q * lax.rsqrt(jnp.sum(q * q, axis=-1, keepdims=True) + EPS) * (dk ** -0.5)
        k = k * lax.rsqrt(jnp.sum(k * k, axis=-1, keepdims=True) + EPS)
        eg = jnp.exp(gcol)
        lmat = jnp.where(strict, bcol * _dot(k, k, NT) * decay, 0.0)
        rhs = jnp.concatenate([k * (bcol * eg), v * bcol], axis=1)
        xs = [rhs[8 * r:8 * r + 8, :] for r in range(ng)]
        for j in range(c - 1):
            row = xs[j // 8][j % 8:j % 8 + 1, :]
            for r in range(j // 8, ng):
                xs[r] = xs[r] - lmat[8 * r:8 * r + 8, j:j + 1] * row
        x_sol = jnp.concatenate(xs, axis=0)
        w = x_sol[:, :dk]
        u = x_sol[:, dk:]
        qk = jnp.where(incl, _dot(q, k, NT) * decay, 0.0)
        glast = gcol[c - 1:c, :]
        state = st_ref[h]
        v_new = u - _dot(w, state)
        o = _dot(q * eg, state) + _dot(qk, v_new)
        st_ref[h] = jnp.exp(glast) * state + _dot(k * jnp.exp(glast - gcol), v_new, TN)
        o = o * lax.rsqrt(jnp.mean(o * o, axis=-1, keepdims=True) + EPS) * ng_ref[...]
        zh = z[:, h * dv:(h + 1) * dv]
        o_ref[:, h * dv:(h + 1) * dv] = (o * (zh * jax.nn.sigmoid(zh))).astype(o_ref.dtype)


def _gdn(proj, qkv_blk, z_blk, sm_blk, bsz, seq, conv_w, a_log, dt_bias, norm_g):
    t = bsz * seq
    c = GDN_CHUNK
    nch = seq // c
    k_len = conv_w.shape[0]
    alog = jnp.zeros((1, LANE), F32).at[0, SM_A:SM_A + GDN_H].set(a_log)
    dtb = jnp.zeros((1, LANE), F32).at[0, SM_A:SM_A + GDN_H].set(dt_bias)
    kern = functools.partial(_gdn_kernel, k_len=k_len)
    full = lambda shape: pl.BlockSpec(shape, lambda b, n: tuple(0 for _ in shape))
    return pl.pallas_call(
        kern,
        grid=(bsz, nch),
        in_specs=[pl.BlockSpec((c, GDN_QKV), lambda b, n: (b * nch + n, qkv_blk)),
                  pl.BlockSpec((c, GDN_WIDTH), lambda b, n: (b * nch + n, z_blk)),
                  pl.BlockSpec((c, LANE), lambda b, n: (b * nch + n, sm_blk)),
                  full((k_len, GDN_QKV)), full((1, LANE)), full((1, LANE)), full((1, GDN_DV))],
        out_specs=pl.BlockSpec((c, GDN_WIDTH), lambda b, n: (b * nch + n, 0)),
        out_shape=jax.ShapeDtypeStruct((t, GDN_WIDTH), BF16),
        scratch_shapes=[pltpu.VMEM((GDN_HALO + c, GDN_QKV), F32),
                        pltpu.VMEM((GDN_H, GDN_DK, GDN_DV), F32)],
        compiler_params=_params("parallel", "arbitrary"),
        name="gdn",
    )(proj, proj, proj, conv_w, alog, dtb, norm_g.reshape(1, -1))


def _merge_kernel(x_ref, mod_ref, yc_ref, yn_ref, yg_ref, gm_ref, wc_ref, wn_ref, wg_ref, wo_ref, o_ref):
    d = x_ref.shape[1]
    gm = jax.nn.sigmoid(gm_ref[...])
    merged = (gm[:, 0:d] * jnp.dot(yc_ref[...], wc_ref[...], preferred_element_type=F32)
              + gm[:, d:2 * d] * jnp.dot(yn_ref[...], wn_ref[...], preferred_element_type=F32)
              + gm[:, 2 * d:3 * d] * jnp.dot(yg_ref[...], wg_ref[...], preferred_element_type=F32))
    gt = mod_ref[0][2:3]
    o_ref[...] = x_ref[...] + gt * _dot(merged, wo_ref[...])


def _merge(x2, mod_l, y_conv, y_nsa, y_gdn, proj, gm_blk, w_c, w_n, w_g, w_o, seq, tm=512):
    t, d = x2.shape
    per_b = seq // tm
    row = lambda w: pl.BlockSpec((tm, w), lambda i: (i, 0))
    wspec = lambda w: pl.BlockSpec(w.shape, lambda i: (0, 0))
    return pl.pallas_call(
        _merge_kernel,
        grid=(t // tm,),
        in_specs=[row(d),
                  pl.BlockSpec((1, N_MOD, d), lambda i: (i // per_b, 0, 0)),
                  row(y_conv.shape[1]), row(y_nsa.shape[1]), row(y_gdn.shape[1]),
                  pl.BlockSpec((tm, 3 * d), lambda i: (i, gm_blk)),
                  wspec(w_c), wspec(w_n), wspec(w_g), wspec(w_o)],
        out_specs=row(d),
        out_shape=jax.ShapeDtypeStruct((t, d), F32),
        compiler_params=_params("parallel"),
        name="merge",
    )(x2, mod_l, y_conv, y_nsa, y_gdn, proj, w_c, w_n, w_g, w_o)


def _ffn_kernel(x_ref, mod_ref, g_ref, wg_ref, wu_ref, wo_ref, fg_ref, o_ref, h_ref, acc_ref, *, final):
    j = pl.program_id(1)

    @pl.when(j == 0)
    def _():
        m = mod_ref[0]
        h_ref[...] = _norm_mod(x_ref[...], g_ref[...], m[3:4], m[4:5]).astype(BF16)
        acc_ref[...] = jnp.zeros(acc_ref.shape, F32)

    h = h_ref[...]
    gate = jnp.dot(h, wg_ref[...], preferred_element_type=F32)
    up = jnp.dot(h, wu_ref[...], preferred_element_type=F32)
    act = (gate * jax.nn.sigmoid(gate)) * up
    acc_ref[...] += _dot(act, wo_ref[...])

    @pl.when(j == pl.num_programs(1) - 1)
    def _():
        y = x_ref[...] + mod_ref[0][5:6] * acc_ref[...]
        if final:
            y = y * lax.rsqrt(jnp.mean(y * y, axis=-1, keepdims=True) + EPS) * fg_ref[...]
        o_ref[...] = y


def _ffn(x2, mod_l, g, w_in, w_out, final_g, seq, final, tm=512, th=1408):
    t, d = x2.shape
    hid = w_out.shape[0]
    per_b = seq // tm
    nh = hid // th
    kern = functools.partial(_ffn_kernel, final=final)
    return pl.pallas_call(
        kern,
        grid=(t // tm, nh),
        in_specs=[pl.BlockSpec((tm, d), lambda i, j: (i, 0)),
                  pl.BlockSpec((1, N_MOD, d), lambda i, j: (i // per_b, 0, 0)),
                  pl.BlockSpec((1, d), lambda i, j: (0, 0)),
                  pl.BlockSpec((d, th), lambda i, j: (0, j)),
                  pl.BlockSpec((d, th), lambda i, j: (0, nh + j)),
                  pl.BlockSpec((th, d), lambda i, j: (j, 0)),
                  pl.BlockSpec((1, d), lambda i, j: (0, 0))],
        out_specs=pl.BlockSpec((tm, d), lambda i, j: (i, 0)),
        out_shape=jax.ShapeDtypeStruct((t, d), F32),
        scratch_shapes=[pltpu.VMEM((tm, d), BF16), pltpu.VMEM((tm, d), F32)],
        compiler_params=_params("parallel", "arbitrary"),
        name="ffn",
    )(x2, mod_l, g, w_in, w_in, w_out, final_g)


def _in_layout(d):
    segs = {}
    off = 0
    for name, width in (("gm", 3 * d), ("conv", 2 * C_CONV), ("q", NSA_WIDTH), ("qkv", GDN_QKV),
                        ("z", GDN_WIDTH), ("small", LANE), ("pad", LANE),
                        ("kv", 6 * NSA_G * NSA_DH)):
        segs[name] = off
        off += width
    return segs, off


def _rearrange_in_proj(w_in, b_in, d):
    splits = (2 * C_CONV, NSA_WIDTH, 6 * NSA_G * NSA_DH, 3 * NSA_HEADS, GDN_QKV, GDN_WIDTH, GDN_H, GDN_H, 3 * d)
    offs = np.concatenate([[0], np.cumsum(splits)])
    names = ("conv", "q", "kv", "gnsa", "qkv", "z", "beta", "a", "gm")
    segs, total = _in_layout(d)

    def arrange(a):
        src = {n: a[..., int(offs[i]):int(offs[i + 1])] for i, n in enumerate(names)}
        lead = a.shape[:-1]
        kv = jnp.swapaxes(src["kv"].reshape(lead + (6, NSA_G, NSA_DH)), -3, -2).reshape(lead + (-1,))
        small_pad = jnp.zeros(lead + (LANE - 3 * NSA_HEADS - 2 * GDN_H,), a.dtype)
        pad = jnp.zeros(lead + (LANE,), a.dtype)
        out = jnp.concatenate([src["gm"], src["conv"], src["q"], src["qkv"], src["z"],
                               src["gnsa"], src["beta"], src["a"], small_pad, pad, kv], axis=-1)
        assert out.shape[-1] == total
        return out

    return arrange(w_in).astype(BF16), arrange(b_in)[:, None, :], segs


def kernel(x, c, positions, ada_w, ada_b, norm_mix_g, norm_ffn_g, w_in, b_in, conv_dw_w, conv_dw_b, conv_ln_g, conv_ln_b, nsa_pe_k, nsa_pe_v, nsa_wk_cmp, nsa_wv_cmp, gdn_conv_w, gdn_a_log, gdn_dt_bias, gdn_norm_g, w_up_conv, w_up_nsa, w_up_gdn, w_o, ffn_w_in, ffn_w_out, final_norm_g):
    bsz, seq, d = x.shape
    depth = ada_w.shape[0]
    t = bsz * seq

    mod = _modulation(c, ada_w, ada_b).reshape(depth, bsz, N_MOD, d)
    cos_t, sin_t = _rope_tables(positions)
    nsa_consts = _nsa_constants(seq)
    w_in_r, b_in_r, segs = _rearrange_in_proj(w_in, b_in, d)
    blk = lambda name, width: segs[name] // width
    qw = NSA_HPG * NSA_DH

    x2 = x.reshape(t, d)
    for l in range(depth):
        proj = _in_projection(x2, mod[l], norm_mix_g[l].reshape(1, d), w_in_r[l], b_in_r[l], seq)
        y_conv = _conformer_conv(proj, blk("conv", 2 * C_CONV), bsz, seq,
                                 conv_dw_w[l], conv_dw_b[l], conv_ln_g[l], conv_ln_b[l])
        y_nsa = _nsa(proj, blk("q", qw), blk("kv", 3 * LANE), blk("small", LANE), bsz, seq, cos_t, sin_t,
                     nsa_pe_k[l], nsa_pe_v[l], nsa_wk_cmp[l], nsa_wv_cmp[l], nsa_consts)
        y_gdn = _gdn(proj, blk("qkv", GDN_QKV), blk("z", GDN_WIDTH), blk("small", LANE), bsz, seq,
                     gdn_conv_w[l], gdn_a_log[l], gdn_dt_bias[l], gdn_norm_g[l])
        x2 = _merge(x2, mod[l], y_conv, y_nsa, y_gdn, proj, blk("gm", 3 * d),
                    w_up_conv[l].astype(BF16), w_up_nsa[l].astype(BF16), w_up_gdn[l].astype(BF16),
                    w_o[l].astype(BF16), seq)
        x2 = _ffn(x2, mod[l], norm_ffn_g[l].reshape(1, d), ffn_w_in[l].astype(BF16),
                  ffn_w_out[l].astype(BF16), final_norm_g.reshape(1, d), seq, final=(l == depth - 1))
    return x2.reshape(bsz, seq, d)
```

```python
import functools

import numpy as np
import jax
import jax.numpy as jnp
from jax import lax
from jax.experimental import pallas as pl
from jax.experimental.pallas import tpu as pltpu

F32 = jnp.float32
BF16 = jnp.bfloat16
HI = lax.Precision.HIGHEST
NT = (((1,), (1,)), ((), ()))
TN = (((0,), (0,)), ((), ()))

EPS = 1e-6
N_MOD = 6
C_CONV = 512
CONV_HALO = 32
NSA_HEADS = 8
NSA_G = 2
NSA_HPG = NSA_HEADS // NSA_G
NSA_DH = 64
NSA_WIDTH = NSA_HEADS * NSA_DH
ROPE_DIM = NSA_DH // 4
ROPE_THETA = 500000.0
L_CMP = 32
STRIDE_CMP = 16
L_SLC = 64
N_SLC_TOP = 8
WINDOW = 512
NSA_TQ = 256
NSA_TK = 128
NSA_KB = 256
GDN_H = 4
GDN_DK = 128
GDN_DV = 128
GDN_WIDTH = GDN_H * GDN_DV
GDN_QKV = GDN_H * (2 * GDN_DK + GDN_DV)
GDN_CHUNK = 64
GDN_HALO = 8
GDN_STEP_CHUNKS = 4
GDN_SOLVE_BLOCK = 16
SM_GATE = 0
SM_BETA = 24
SM_A = 28
LANE = 128

VMEM_LIMIT = 56 * 1024 * 1024


def _dot(a, b, dims=None, hi=False):
    if dims is None:
        dims = (((a.ndim - 1,), (0,)), ((), ()))
    if hi:
        return lax.dot_general(a.astype(F32), b.astype(F32), dims, precision=HI,
                               preferred_element_type=F32)
    return lax.dot_general(a.astype(BF16), b.astype(BF16), dims, preferred_element_type=F32)


def _masked_softmax0(s, mask):
    s = jnp.where(mask, s, -1e30)
    m = jnp.max(s, axis=0, keepdims=True)
    p = jnp.where(mask, jnp.exp(s - m), 0.0)
    return p * (1.0 / jnp.maximum(jnp.sum(p, axis=0, keepdims=True), 1e-30))


def _params(*sem):
    return pltpu.CompilerParams(dimension_semantics=sem, vmem_limit_bytes=VMEM_LIMIT)


def _mod_kernel(c_ref, w_ref, b_ref, o_ref):
    c = c_ref[...]
    ca = c * jax.nn.sigmoid(c)
    o_ref[0] = _dot(ca, w_ref[0], hi=True) + b_ref[0]


def _modulation(c, ada_w, ada_b):
    depth, d, n = ada_w.shape
    bsz = c.shape[0]
    tn = 1536
    return pl.pallas_call(
        _mod_kernel,
        grid=(depth, n // tn),
        in_specs=[pl.BlockSpec((bsz, d), lambda l, j: (0, 0)),
                  pl.BlockSpec((1, d, tn), lambda l, j: (l, 0, j)),
                  pl.BlockSpec((1, 1, tn), lambda l, j: (l, 0, j))],
        out_specs=pl.BlockSpec((1, bsz, tn), lambda l, j: (l, 0, j)),
        out_shape=jax.ShapeDtypeStruct((depth, bsz, n), F32),
        compiler_params=_params("parallel", "parallel"),
        name="adaln_mod",
    )(c, ada_w, ada_b.reshape(depth, 1, n))


def _norm_mod(x, g, sh, sc):
    y = x * lax.rsqrt(jnp.mean(x * x, axis=-1, keepdims=True) + EPS)
    return (y * g) * (1.0 + sc) + sh


def _inproj_kernel(x_ref, mod_ref, g_ref, w_ref, b_ref, o_ref, h_ref):
    @pl.when(pl.program_id(1) == 0)
    def _():
        m = mod_ref[0]
        h_ref[...] = _norm_mod(x_ref[...], g_ref[...], m[0:1], m[1:2]).astype(BF16)

    o_ref[...] = jnp.dot(h_ref[...], w_ref[...], preferred_element_type=F32) + b_ref[...]


def _in_projection(x2, mod_l, g, w, b, seq, tm=1024, tn=1536):
    t, d = x2.shape
    n = w.shape[1]
    per_b = seq // tm
    return pl.pallas_call(
        _inproj_kernel,
        grid=(t // tm, n // tn),
        in_specs=[pl.BlockSpec((tm, d), lambda i, j: (i, 0)),
                  pl.BlockSpec((1, N_MOD, d), lambda i, j: (i // per_b, 0, 0)),
                  pl.BlockSpec((1, d), lambda i, j: (0, 0)),
                  pl.BlockSpec((d, tn), lambda i, j: (0, j)),
                  pl.BlockSpec((1, tn), lambda i, j: (0, j))],
        out_specs=pl.BlockSpec((tm, tn), lambda i, j: (i, j)),
        out_shape=jax.ShapeDtypeStruct((t, n), F32),
        scratch_shapes=[pltpu.VMEM((tm, d), BF16)],
        compiler_params=_params("parallel", "arbitrary"),
        name="in_proj",
    )(x2, mod_l, g, w, b)


def _conv_kernel(u_ref, w_ref, b_ref, lg_ref, lb_ref, o_ref, buf_ref, *, ts, k_len, rc):
    c = C_CONV

    @pl.when(pl.program_id(1) == 0)
    def _():
        buf_ref[0:CONV_HALO, :] = jnp.zeros((CONV_HALO, c), F32)

    @pl.when(pl.program_id(1) > 0)
    def _():
        buf_ref[0:CONV_HALO, :] = buf_ref[ts:ts + CONV_HALO, :]

    u = u_ref[...]
    buf_ref[CONV_HALO:CONV_HALO + ts, :] = u[:, :c] * jax.nn.sigmoid(u[:, c:])
    base = CONV_HALO - (k_len - 1)

    def chunk(ci, carry):
        r0 = pl.multiple_of(ci * rc, rc)
        win = buf_ref[pl.ds(r0, rc + CONV_HALO), :]
        acc = jnp.zeros((rc // 8, 8, c), F32)
        for ph in range(8):
            taps = [k for k in range(k_len) if (base + k) % 8 == ph]
            if not taps:
                continue
            span = max(base + k for k in taps) - ph + rc
            shifted = win[ph:ph + span, :]
            for k in taps:
                off = base + k - ph
                acc = acc + w_ref[k][None] * shifted[off:off + rc, :].reshape(rc // 8, 8, c)
        y = acc.reshape(rc, c) + b_ref[...]
        mu = jnp.mean(y, axis=-1, keepdims=True)
        var = jnp.mean(jnp.square(y - mu), axis=-1, keepdims=True)
        y = (y - mu) * lax.rsqrt(var + EPS) * lg_ref[...] + lb_ref[...]
        o_ref[pl.ds(r0, rc), :] = (y * jax.nn.sigmoid(y)).astype(o_ref.dtype)
        return carry

    lax.fori_loop(0, ts // rc, chunk, 0)


def _conformer_conv(proj, col_blk, bsz, seq, dw_w, dw_b, ln_g, ln_b, ts=512, rc=32):
    t = bsz * seq
    k_len = dw_w.shape[0]
    per_b = seq // ts
    kern = functools.partial(_conv_kernel, ts=ts, k_len=k_len, rc=rc)
    vec = lambda: pl.BlockSpec((1, C_CONV), lambda b, s: (0, 0))
    return pl.pallas_call(
        kern,
        grid=(bsz, per_b),
        in_specs=[pl.BlockSpec((ts, 2 * C_CONV), lambda b, s: (b * per_b + s, col_blk)),
                  pl.BlockSpec((k_len, 8, C_CONV), lambda b, s: (0, 0, 0)),
                  vec(), vec(), vec()],
        out_specs=pl.BlockSpec((ts, C_CONV), lambda b, s: (b * per_b + s, 0)),
        out_shape=jax.ShapeDtypeStruct((t, C_CONV), BF16),
        scratch_shapes=[pltpu.VMEM((CONV_HALO + ts, C_CONV), F32)],
        compiler_params=_params("parallel", "arbitrary"),
        name="conformer_conv",
    )(proj, jnp.broadcast_to(dw_w[:, None, :], (k_len, 8, C_CONV)),
      dw_b.reshape(1, -1), ln_g.reshape(1, -1), ln_b.reshape(1, -1))


def _rope_table_kernel(pos_ref, inv_ref, sign_ref, cos_ref, sin_ref):
    ang = pos_ref[0].astype(F32) * inv_ref[...]
    cos_ref[0] = jnp.cos(ang)
    sin_ref[0] = jnp.sin(ang) * sign_ref[...]


def _rope_tables(positions):
    bsz, seq = positions.shape
    half = ROPE_DIM // 2
    inv = ROPE_THETA ** (-jnp.arange(half, dtype=F32) * 2.0 / ROPE_DIM)
    d = np.arange(LANE) % NSA_DH
    inv_lane = jnp.where(jnp.asarray(d < ROPE_DIM), inv[d % half], 0.0).reshape(1, LANE)
    sign_lane = jnp.asarray(np.where(d < half, -1.0, np.where(d < ROPE_DIM, 1.0, 0.0)),
                            dtype=F32).reshape(1, LANE)
    out = jax.ShapeDtypeStruct((bsz, seq, LANE), F32)
    return pl.pallas_call(
        _rope_table_kernel,
        grid=(bsz,),
        in_specs=[pl.BlockSpec((1, seq, 1), lambda b: (b, 0, 0)),
                  pl.BlockSpec((1, LANE), lambda b: (0, 0)),
                  pl.BlockSpec((1, LANE), lambda b: (0, 0))],
        out_specs=[pl.BlockSpec((1, seq, LANE), lambda b: (b, 0, 0))] * 2,
        out_shape=[out, out],
        compiler_params=_params("parallel"),
        name="rope_tables",
    )(positions.reshape(bsz, seq, 1), inv_lane, sign_lane)


def _rope_apply(x, cos_t, sin_t, width):
    lane = lax.broadcasted_iota(jnp.int32, (1, width), 1)
    lo = (lane % NSA_DH) < (ROPE_DIM // 2)
    partner = jnp.where(lo, pltpu.roll(x, width - ROPE_DIM // 2, 1), pltpu.roll(x, ROPE_DIM // 2, 1))
    return x * cos_t + partner * sin_t


def _nsa_kernel(q_ref, kv_ref, sm_ref, cos_ref, sin_ref, pe_ref, wc_ref, ovl_ref, eexp_ref, gsel_ref,
                o_ref,
                qt_ref, csrc_ref, ccat_ref, kc_ref, vct_ref, kslc_ref, vslct_ref, kwin_ref, vwint_ref,
                mfull_ref, *, seq):
    tq, tk, kb, dh, hpg = NSA_TQ, NSA_TK, NSA_KB, NSA_DH, NSA_HPG
    nc = seq // STRIDE_CMP
    ns = seq // L_SLC
    nwb = WINDOW // tk
    band = WINDOW + tq
    scale = dh ** -0.5
    qw = hpg * dh
    n_top = float(min(N_SLC_TOP, ns))
    lane = lax.broadcasted_iota(jnp.int32, (1, LANE), 1)

    csrc_ref[seq:seq + L_CMP, :] = jnp.zeros((L_CMP, LANE), F32)
    kwin_ref[0:WINDOW, :] = jnp.zeros((WINDOW, dh), BF16)
    vwint_ref[0:nwb] = jnp.zeros((nwb, dh, tk), BF16)

    def prep(blk, carry):
        r0 = pl.multiple_of(blk * tk, tk)
        cos_t = cos_ref[0, pl.ds(r0, tk), :]
        sin_t = sin_ref[0, pl.ds(r0, tk), :]
        cq = jnp.concatenate([cos_t] * (qw // LANE), axis=1)
        sq = jnp.concatenate([sin_t] * (qw // LANE), axis=1)
        qr = _rope_apply(q_ref[pl.ds(r0, tk), :], cq, sq, qw) * scale
        qt_ref[blk] = qr.T.astype(BF16)
        ck = jnp.where(lane < dh, cos_t, 1.0)
        sk = jnp.where(lane < dh, sin_t, 0.0)
        kv = kv_ref[pl.ds(r0, tk), :]
        csrc_ref[pl.ds(r0, tk), :] = _rope_apply(kv[:, 0:LANE], ck, sk, LANE)
        kv_s = _rope_apply(kv[:, LANE:2 * LANE], ck, sk, LANE)
        kslc_ref[pl.ds(r0, tk), :] = kv_s[:, :dh].astype(BF16)
        vslct_ref[blk] = kv_s.T[dh:, :].astype(BF16)
        kv_w = _rope_apply(kv[:, 2 * LANE:3 * LANE], ck, sk, LANE)
        kwin_ref[pl.ds(pl.multiple_of(WINDOW + r0, tk), tk), :] = kv_w[:, :dh].astype(BF16)
        vwint_ref[nwb + blk] = kv_w.T[dh:, :].astype(BF16)
        return carry

    lax.fori_loop(0, seq // tk, prep, 0)

    for l in range(L_CMP):
        ccat_ref[:, l * LANE:(l + 1) * LANE] = (
            csrc_ref[pl.ds(l, nc, stride=STRIDE_CMP), :] + pe_ref[l:l + 1, :])
    kvc = _dot(ccat_ref[...], wc_ref[...])
    kc_ref[...] = kvc[:, :dh].astype(BF16)
    vct_ref[...] = kvc.T[dh:, :].astype(BF16)

    n_sub = lax.broadcasted_iota(jnp.int32, (nc, 1), 0)
    j_sub = lax.broadcasted_iota(jnp.int32, (ns, 1), 0)
    k_sub = lax.broadcasted_iota(jnp.int32, (kb, 1), 0)
    c_sub = lax.broadcasted_iota(jnp.int32, (band, 1), 0)
    q_lane = lax.broadcasted_iota(jnp.int32, (1, tq), 1)
    win_static = (c_sub - WINDOW <= q_lane) & (c_sub > q_lane)
    heads = [slice(h * tq, (h + 1) * tq) for h in range(hpg)]

    def q_tile(i, carry):
        r0 = pl.multiple_of(i * tq, tq)
        b0 = i * (tq // tk)
        qblk = [qt_ref[b0 + jb] for jb in range(tq // tk)]
        qt4 = jnp.concatenate([qb[h * dh:(h + 1) * dh, :] for h in range(hpg) for qb in qblk],
                              axis=1)
        t_lane = r0 + q_lane

        s = _dot(kc_ref[...], qt4)
        cmask = (n_sub * STRIDE_CMP + (L_CMP - 1)) <= t_lane
        ps = [_masked_softmax0(s[:, sl], cmask) for sl in heads]
        o_cmp = _dot(vct_ref[...], jnp.concatenate([p.astype(BF16) for p in ps], axis=1))
        psum = ps[0]
        for p in ps[1:]:
            psum = psum + p

        imp = _dot(ovl_ref[...], psum, hi=True)
        cur = t_lane // L_SLC
        causal_blk = j_sub <= cur
        forced = (j_sub == 0) | (j_sub == cur) | (j_sub == cur - 1)
        impv = jnp.where(forced, 1e9, jnp.where(causal_blk, imp, -1e9))
        rank = jnp.zeros((ns, tq), F32)
        for b in range(ns):
            row = impv[b:b + 1, :]
            beats = (row > impv) | ((row == impv) & (b < j_sub))
            rank = rank + jnp.where(beats, 1.0, 0.0)
        sel = jnp.where((rank < n_top) & causal_blk, 1.0, 0.0)
        mfull_ref[...] = _dot(eexp_ref[...], sel)

        s = _dot(kwin_ref[pl.ds(r0, band), :], qt4)
        wmask = win_static & (c_sub >= WINDOW - r0)
        p4 = jnp.concatenate([_masked_softmax0(s[:, sl], wmask).astype(BF16) for sl in heads], axis=1)
        o_win = _dot(vwint_ref[b0], p4[0:tk, :])
        for jb in range(1, band // tk):
            o_win = o_win + _dot(vwint_ref[b0 + jb], p4[jb * tk:(jb + 1) * tk, :])

        gt = _dot(gsel_ref[0], jax.nn.sigmoid(sm_ref[pl.ds(r0, tq), :]), NT, hi=True)
        gate = lambda br: jnp.concatenate(
            [jnp.broadcast_to(gt[br * hpg + h:br * hpg + h + 1], (dh, tq)) for h in range(hpg)], axis=1)
        o_cw = gate(0) * o_cmp + gate(2) * o_win
        g_slc = gate(1)

        def kv_step(j, st):
            m, l, acc = st
            k0 = pl.multiple_of(j * kb, kb)
            s = _dot(kslc_ref[pl.ds(k0, kb), :], qt4)
            msk = (mfull_ref[pl.ds(k0, kb), :] > 0.5) & (k0 + k_sub <= t_lane)
            ms, ls, als, ps = [], [], [], []
            for sl in heads:
                sh = jnp.where(msk, s[:, sl], -1e30)
                mn = jnp.maximum(m[:, sl], jnp.max(sh, axis=0, keepdims=True))
                al = jnp.exp(m[:, sl] - mn)
                p = jnp.where(msk, jnp.exp(sh - mn), 0.0)
                ms.append(mn)
                als.append(al)
                ls.append(al * l[:, sl] + jnp.sum(p, axis=0, keepdims=True))
                ps.append(p.astype(BF16))
            p4 = jnp.concatenate(ps, axis=1)
            pv = _dot(vslct_ref[j * (kb // tk)], p4[0:tk, :])
            for hf in range(1, kb // tk):
                pv = pv + _dot(vslct_ref[j * (kb // tk) + hf], p4[hf * tk:(hf + 1) * tk, :])
            return (jnp.concatenate(ms, axis=1), jnp.concatenate(ls, axis=1),
                    jnp.concatenate(als, axis=1) * acc + pv)

        init = (jnp.full((1, hpg * tq), -1e30, F32), jnp.zeros((1, hpg * tq), F32),
                jnp.zeros((dh, hpg * tq), F32))
        _, l_s, acc_s = lax.fori_loop(0, (r0 + tq + kb - 1) // kb, kv_step, init)
        o_slc = acc_s * (1.0 / jnp.maximum(l_s, 1e-30))

        out_t = o_cw + g_slc * o_slc
        out_t = jnp.concatenate([out_t[:, sl] for sl in heads], axis=0)
        o_ref[pl.ds(r0, tq), :] = out_t.T.astype(o_ref.dtype)
        return carry

    lax.fori_loop(0, seq // tq, q_tile, 0)


def _nsa_constants(seq):
    nc = seq // STRIDE_CMP
    ns = seq // L_SLC
    cs = np.arange(nc) * STRIDE_CMP
    js = np.arange(ns) * L_SLC
    ovl_t = ((cs[None, :] < js[:, None] + L_SLC) & (cs[None, :] + L_CMP > js[:, None])).astype(np.float32)
    eexp_t = (np.arange(seq)[:, None] // L_SLC == np.arange(ns)[None, :]).astype(np.float32)
    gsel = np.zeros((NSA_G, 16, LANE), np.float32)
    for g in range(NSA_G):
        for h in range(NSA_HPG):
            for j in range(3):
                gsel[g, j * NSA_HPG + h, SM_GATE + (g * NSA_HPG + h) * 3 + j] = 1.0
    return jnp.asarray(ovl_t), jnp.asarray(eexp_t, dtype=BF16), jnp.asarray(gsel)


def _nsa(proj, q_blk, kv_blk, sm_blk, bsz, seq, cos_t, sin_t, pe_k, pe_v, wk_cmp, wv_cmp, consts):
    t = bsz * seq
    dh = NSA_DH
    qw = NSA_HPG * dh
    tq, tk = NSA_TQ, NSA_TK
    nc = seq // STRIDE_CMP
    ns = seq // L_SLC
    ovl_t, eexp_t, gsel = consts
    pe = jnp.concatenate([pe_k, pe_v], axis=1)
    zero = jnp.zeros((L_CMP, dh, dh), F32)
    wc = jnp.concatenate([jnp.concatenate([wk_cmp, zero], axis=2),
                          jnp.concatenate([zero, wv_cmp], axis=2)], axis=1)
    wc = wc.reshape(L_CMP * LANE, LANE).astype(BF16)
    kern = functools.partial(_nsa_kernel, seq=seq)
    full = lambda shape: pl.BlockSpec(shape, lambda b, g: tuple(0 for _ in shape))
    return pl.pallas_call(
        kern,
        grid=(bsz, NSA_G),
        in_specs=[pl.BlockSpec((seq, qw), lambda b, g: (b, q_blk + g)),
                  pl.BlockSpec((seq, 3 * LANE), lambda b, g: (b, kv_blk + g)),
                  pl.BlockSpec((seq, LANE), lambda b, g: (b, sm_blk)),
                  pl.BlockSpec((1, seq, LANE), lambda b, g: (b, 0, 0)),
                  pl.BlockSpec((1, seq, LANE), lambda b, g: (b, 0, 0)),
                  full((L_CMP, LANE)),
                  full((L_CMP * LANE, LANE)),
                  full((ns, nc)),
                  full((seq, ns)),
                  pl.BlockSpec((1, 16, LANE), lambda b, g: (g, 0, 0))],
        out_specs=pl.BlockSpec((seq, qw), lambda b, g: (b, g)),
        out_shape=jax.ShapeDtypeStruct((t, NSA_G * qw), BF16),
        scratch_shapes=[pltpu.VMEM((seq // tk, qw, tk), BF16),
                        pltpu.VMEM((seq + L_CMP, LANE), F32),
                        pltpu.VMEM((nc, L_CMP * LANE), F32),
                        pltpu.VMEM((nc, dh), BF16),
                        pltpu.VMEM((dh, nc), BF16),
                        pltpu.VMEM((seq, dh), BF16),
                        pltpu.VMEM((seq // tk, dh, tk), BF16),
                        pltpu.VMEM((WINDOW + seq, dh), BF16),
                        pltpu.VMEM(((WINDOW + seq) // tk, dh, tk), BF16),
                        pltpu.VMEM((seq, tq), F32)],
        compiler_params=_params("parallel", "parallel"),
        name="nsa",
    )(proj, proj, proj, cos_t, sin_t, pe, wc, ovl_t, eexp_t, gsel)


def _dot3(a, b):
    a_hi = a.astype(BF16)
    a_lo = (a - a_hi.astype(F32)).astype(BF16)
    b_hi = b.astype(BF16)
    b_lo = (b - b_hi.astype(F32)).astype(BF16)
    d = lambda x, y: jnp.dot(x, y, preferred_element_type=F32)
    return d(a_hi, b_hi) + (d(a_lo, b_hi) + d(a_hi, b_lo))


def _unit_lower_solve(lmats, rhss):
    c, width = rhss[0].shape
    bs = GDN_SOLVE_BLOCK
    done = [[] for _ in rhss]
    for b in range(c // bs):
        r0 = b * bs
        xbs = []
        for lmat, rhs, dn in zip(lmats, rhss, done):
            xb = rhs[r0:r0 + bs]
            if b:
                partial = jnp.concatenate(dn + [jnp.zeros((c - r0, width), F32)], axis=0)
                xb = xb - _dot3(lmat[r0:r0 + bs, :], partial)
            xbs.append(xb)
        for lmat, xb, dn in zip(lmats, xbs, done):
            lb = lmat[r0:r0 + bs, r0:r0 + bs]
            groups = [xb[8 * g:8 * g + 8] for g in range(bs // 8)]
            for j in range(bs - 1):
                row = groups[j // 8][j % 8:j % 8 + 1, :]
                for g in range(j // 8, bs // 8):
                    if 8 * g + 7 > j:
                        groups[g] = groups[g] - lb[8 * g:8 * g + 8, j:j + 1] * row
            dn.append(jnp.concatenate(groups, axis=0))
    return [jnp.concatenate(dn, axis=0) for dn in done]


def _gdn_kernel(qkv_ref, z_ref, sm_ref, cw_ref, alog_ref, dtb_ref, ng_ref, o_ref, buf_ref, xc_ref, st_ref,
                *, k_len):
    c = GDN_CHUNK
    rows = c * GDN_STEP_CHUNKS
    dk, dv = GDN_DK, GDN_DV

    @pl.when(pl.program_id(1) == 0)
    def _():
        buf_ref[0:GDN_HALO, :] = jnp.zeros((GDN_HALO, GDN_QKV), F32)
        st_ref[...] = jnp.zeros(st_ref.shape, F32)

    x = qkv_ref[...]
    buf_ref[GDN_HALO:GDN_HALO + rows, :] = x
    base = GDN_HALO - (k_len - 1)
    acc = jnp.zeros((rows, GDN_QKV), F32)
    for k in range(k_len):
        acc = acc + cw_ref[k:k + 1, :] * buf_ref[base + k:base + k + rows, :]
    buf_ref[0:GDN_HALO, :] = x[rows - GDN_HALO:rows, :]
    xc_ref[...] = acc * jax.nn.sigmoid(acc)

    sm = sm_ref[...]
    beta_all = jax.nn.sigmoid(sm)
    g_all = -jnp.exp(alog_ref[...]) * jax.nn.softplus(sm + dtb_ref[...])
    ri = lax.broadcasted_iota(jnp.int32, (rows, rows), 0)
    rj = lax.broadcasted_iota(jnp.int32, (rows, rows), 1)
    same_chunk_incl = (ri >= rj) & (ri // c == rj // c)
    gc_all = _dot(jnp.where(same_chunk_incl, 1.0, 0.0), g_all, hi=True)
    gc_rows = gc_all.T
    ii = lax.broadcasted_iota(jnp.int32, (c, c), 0)
    jj = lax.broadcasted_iota(jnp.int32, (c, c), 1)
    incl = ii >= jj
    strict = ii > jj

    pre, lmats, rhss = [], [], []
    for n in range(GDN_STEP_CHUNKS):
        rs = slice(n * c, (n + 1) * c)
        for h in range(GDN_H):
            gcol = gc_all[rs, SM_A + h:SM_A + h + 1]
            bcol = beta_all[rs, SM_BETA + h:SM_BETA + h + 1]
            grow = gc_rows[SM_A + h:SM_A + h + 1, rs]
            decay = jnp.exp(jnp.where(incl, gcol - grow, -jnp.inf))
            q = xc_ref[rs, h * dk:(h + 1) * dk]
            k = xc_ref[rs, GDN_H * dk + h * dk:GDN_H * dk + (h + 1) * dk]
            v = xc_ref[rs, 2 * GDN_H * dk + h * dv:2 * GDN_H * dk + (h + 1) * dv]
            q = q * lax.rsqrt(jnp.sum(q * q, axis=-1, keepdims=True) + EPS) * (dk ** -0.5)
            k = k * lax.rsqrt(jnp.sum(k * k, axis=-1, keepdims=True) + EPS)
            eg = jnp.exp(gcol)
            lmat = jnp.where(strict, bcol * _dot(k, k, NT) * decay, 0.0)
            qk = jnp.where(incl, _dot(q, k, NT) * decay, 0.0)
            glast = gcol[c - 1:c, :]
            lmats.append(lmat)
            rhss.append(jnp.concatenate([k * (bcol * eg), v * bcol], axis=1))
            pre.append((q * eg, k * jnp.exp(glast - gcol), qk, jnp.exp(glast)))
    sols = _unit_lower_solve(lmats, rhss)

    z = z_ref[...]
    for n in range(GDN_STEP_CHUNKS):
        rs = slice(n * c, (n + 1) * c)
        idx = range(n * GDN_H, (n + 1) * GDN_H)
        states = [st_ref[h] for h in range(GDN_H)]
        v_news = [sols[i][:, dk:] - _dot(sols[i][:, :dk], st) for i, st in zip(idx, states)]
        o_state = [_dot(pre[i][0], st) for i, st in zip(idx, states)]
        o_intra = [_dot(pre[i][2], vn) for i, vn in zip(idx, v_news)]
        s_add = [_dot(pre[i][1], vn, TN) for i, vn in zip(idx, v_news)]
        for h, i in enumerate(idx):
            st_ref[h] = pre[i][3] * states[h] + s_add[h]
            o = o_state[h] + o_intra[h]
            o = o * lax.rsqrt(jnp.mean(o * o, axis=-1, keepdims=True) + EPS) * ng_ref[...]
            zh = z[rs, h * dv:(h + 1) * dv]
            o_ref[rs, h * dv:(h + 1) * dv] = (o * (zh * jax.nn.sigmoid(zh))).astype(o_ref.dtype)


def _gdn(proj, qkv_blk, z_blk, sm_blk, bsz, seq, conv_w, a_log, dt_bias, norm_g):
    t = bsz * seq
    c = GDN_CHUNK * GDN_STEP_CHUNKS
    nch = seq // c
    k_len = conv_w.shape[0]
    alog = jnp.zeros((1, LANE), F32).at[0, SM_A:SM_A + GDN_H].set(a_log)
    dtb = jnp.zeros((1, LANE), F32).at[0, SM_A:SM_A + GDN_H].set(dt_bias)
    kern = functools.partial(_gdn_kernel, k_len=k_len)
    full = lambda shape: pl.BlockSpec(shape, lambda b, n: tuple(0 for _ in shape))
    return pl.pallas_call(
        kern,
        grid=(bsz, nch),
        in_specs=[pl.BlockSpec((c, GDN_QKV), lambda b, n: (b * nch + n, qkv_blk)),
                  pl.BlockSpec((c, GDN_WIDTH), lambda b, n: (b * nch + n, z_blk)),
                  pl.BlockSpec((c, LANE), lambda b, n: (b * nch + n, sm_blk)),
                  full((k_len, GDN_QKV)), full((1, LANE)), full((1, LANE)), full((1, GDN_DV))],
        out_specs=pl.BlockSpec((c, GDN_WIDTH), lambda b, n: (b * nch + n, 0)),
        out_shape=jax.ShapeDtypeStruct((t, GDN_WIDTH), BF16),
        scratch_shapes=[pltpu.VMEM((GDN_HALO + c, GDN_QKV), F32),
                        pltpu.VMEM((c, GDN_QKV), F32),
                        pltpu.VMEM((GDN_H, GDN_DK, GDN_DV), F32)],
        compiler_params=_params("parallel", "arbitrary"),
        name="gdn",
    )(proj, proj, proj, conv_w, alog, dtb, norm_g.reshape(1, -1))


def _merge_kernel(x_ref, mod_ref, yc_ref, yn_ref, yg_ref, gm_ref, wc_ref, wn_ref, wg_ref, wo_ref, o_ref):
    d = x_ref.shape[1]
    gm = jax.nn.sigmoid(gm_ref[...])
    merged = (gm[:, 0:d] * jnp.dot(yc_ref[...], wc_ref[...], preferred_element_type=F32)
              + gm[:, d:2 * d] * jnp.dot(yn_ref[...], wn_ref[...], preferred_element_type=F32)
              + gm[:, 2 * d:3 * d] * jnp.dot(yg_ref[...], wg_ref[...], preferred_element_type=F32))
    gt = mod_ref[0][2:3]
    o_ref[...] = x_ref[...] + gt * _dot(merged, wo_ref[...])


def _merge(x2, mod_l, y_conv, y_nsa, y_gdn, proj, gm_blk, w_c, w_n, w_g, w_o, seq, tm=512):
    t, d = x2.shape
    per_b = seq // tm
    row = lambda w: pl.BlockSpec((tm, w), lambda i: (i, 0))
    wspec = lambda w: pl.BlockSpec(w.shape, lambda i: (0, 0))
    return pl.pallas_call(
        _merge_kernel,
        grid=(t // tm,),
        in_specs=[row(d),
                  pl.BlockSpec((1, N_MOD, d), lambda i: (i // per_b, 0, 0)),
                  row(y_conv.shape[1]), row(y_nsa.shape[1]), row(y_gdn.shape[1]),
                  pl.BlockSpec((tm, 3 * d), lambda i: (i, gm_blk)),
                  wspec(w_c), wspec(w_n), wspec(w_g), wspec(w_o)],
        out_specs=row(d),
        out_shape=jax.ShapeDtypeStruct((t, d), F32),
        compiler_params=_params("parallel"),
        name="merge",
    )(x2, mod_l, y_conv, y_nsa, y_gdn, proj, w_c, w_n, w_g, w_o)


def _ffn_kernel(x_ref, mod_ref, g_ref, wg_ref, wu_ref, wo_ref, fg_ref, o_ref, h_ref, acc_ref, *, final):
    j = pl.program_id(1)

    @pl.when(j == 0)
    def _():
        m = mod_ref[0]
        h_ref[...] = _norm_mod(x_ref[...], g_ref[...], m[3:4], m[4:5]).astype(BF16)
        acc_ref[...] = jnp.zeros(acc_ref.shape, F32)

    h = h_ref[...]
    gate = jnp.dot(h, wg_ref[...], preferred_element_type=F32)
    up = jnp.dot(h, wu_ref[...], preferred_element_type=F32)
    act = (gate * jax.nn.sigmoid(gate)) * up
    acc_ref[...] += _dot(act, wo_ref[...])

    @pl.when(j == pl.num_programs(1) - 1)
    def _():
        y = x_ref[...] + mod_ref[0][5:6] * acc_ref[...]
        if final:
            y = y * lax.rsqrt(jnp.mean(y * y, axis=-1, keepdims=True) + EPS) * fg_ref[...]
        o_ref[...] = y


def _ffn(x2, mod_l, g, w_in, w_out, final_g, seq, final, tm=512, th=1408):
    t, d = x2.shape
    hid = w_out.shape[0]
    per_b = seq // tm
    nh = hid // th
    kern = functools.partial(_ffn_kernel, final=final)
    return pl.pallas_call(
        kern,
        grid=(t // tm, nh),
        in_specs=[pl.BlockSpec((tm, d), lambda i, j: (i, 0)),
                  pl.BlockSpec((1, N_MOD, d), lambda i, j: (i // per_b, 0, 0)),
                  pl.BlockSpec((1, d), lambda i, j: (0, 0)),
                  pl.BlockSpec((d, th), lambda i, j: (0, j)),
                  pl.BlockSpec((d, th), lambda i, j: (0, nh + j)),
                  pl.BlockSpec((th, d), lambda i, j: (j, 0)),
                  pl.BlockSpec((1, d), lambda i, j: (0, 0))],
        out_specs=pl.BlockSpec((tm, d), lambda i, j: (i, 0)),
        out_shape=jax.ShapeDtypeStruct((t, d), F32),
        scratch_shapes=[pltpu.VMEM((tm, d), BF16), pltpu.VMEM((tm, d), F32)],
        compiler_params=_params("parallel", "arbitrary"),
        name="ffn",
    )(x2, mod_l, g, w_in, w_in, w_out, final_g)


def _in_layout(d):
    segs = {}
    off = 0
    for name, width in (("gm", 3 * d), ("conv", 2 * C_CONV), ("q", NSA_WIDTH), ("qkv", GDN_QKV),
                        ("z", GDN_WIDTH), ("small", LANE), ("pad", LANE),
                        ("kv", 6 * NSA_G * NSA_DH)):
        segs[name] = off
        off += width
    return segs, off


def _rearrange_in_proj(w_in, b_in, d):
    splits = (2 * C_CONV, NSA_WIDTH, 6 * NSA_G * NSA_DH, 3 * NSA_HEADS, GDN_QKV, GDN_WIDTH, GDN_H, GDN_H, 3 * d)
    offs = np.concatenate([[0], np.cumsum(splits)])
    names = ("conv", "q", "kv", "gnsa", "qkv", "z", "beta", "a", "gm")
    segs, total = _in_layout(d)

    def arrange(a):
        src = {n: a[..., int(offs[i]):int(offs[i + 1])] for i, n in enumerate(names)}
        lead = a.shape[:-1]
        kv = jnp.swapaxes(src["kv"].reshape(lead + (6, NSA_G, NSA_DH)), -3, -2).reshape(lead + (-1,))
        small_pad = jnp.zeros(lead + (LANE - 3 * NSA_HEADS - 2 * GDN_H,), a.dtype)
        pad = jnp.zeros(lead + (LANE,), a.dtype)
        out = jnp.concatenate([src["gm"], src["conv"], src["q"], src["qkv"], src["z"],
                               src["gnsa"], src["beta"], src["a"], small_pad, pad, kv], axis=-1)
        assert out.shape[-1] == total
        return out

    return arrange(w_in).astype(BF16), arrange(b_in)[:, None, :], segs


def kernel(x, c, positions, ada_w, ada_b, norm_mix_g, norm_ffn_g, w_in, b_in, conv_dw_w, conv_dw_b, conv_ln_g, conv_ln_b, nsa_pe_k, nsa_pe_v, nsa_wk_cmp, nsa_wv_cmp, gdn_conv_w, gdn_a_log, gdn_dt_bias, gdn_norm_g, w_up_conv, w_up_nsa, w_up_gdn, w_o, ffn_w_in, ffn_w_out, final_norm_g):
    bsz, seq, d = x.shape
    depth = ada_w.shape[0]
    t = bsz * seq

    mod = _modulation(c, ada_w, ada_b).reshape(depth, bsz, N_MOD, d)
    cos_t, sin_t = _rope_tables(positions)
    nsa_consts = _nsa_constants(seq)
    w_in_r, b_in_r, segs = _rearrange_in_proj(w_in, b_in, d)
    blk = lambda name, width: segs[name] // width
    qw = NSA_HPG * NSA_DH

    x2 = x.reshape(t, d)
    for l in range(depth):
        proj = _in_projection(x2, mod[l], norm_mix_g[l].reshape(1, d), w_in_r[l], b_in_r[l], seq)
        y_conv = _conformer_conv(proj, blk("conv", 2 * C_CONV), bsz, seq,
                                 conv_dw_w[l], conv_dw_b[l], conv_ln_g[l], conv_ln_b[l])
        y_nsa = _nsa(proj, blk("q", qw), blk("kv", 3 * LANE), blk("small", LANE), bsz, seq, cos_t, sin_t,
                     nsa_pe_k[l], nsa_pe_v[l], nsa_wk_cmp[l], nsa_wv_cmp[l], nsa_consts)
        y_gdn = _gdn(proj, blk("qkv", GDN_QKV), blk("z", GDN_WIDTH), blk("small", LANE), bsz, seq,
                     gdn_conv_w[l], gdn_a_log[l], gdn_dt_bias[l], gdn_norm_g[l])
        x2 = _merge(x2, mod[l], y_conv, y_nsa, y_gdn, proj, blk("gm", 3 * d),
                    w_up_conv[l].astype(BF16), w_up_nsa[l].astype(BF16), w_up_gdn[l].astype(BF16),
                    w_o[l].astype(BF16), seq)
        x2 = _ffn(x2, mod[l], norm_ffn_g[l].reshape(1, d), ffn_w_in[l].astype(BF16),
                  ffn_w_out[l].astype(BF16), final_norm_g.reshape(1, d), seq, final=(l == depth - 1))
    return x2.reshape(bsz, seq, d)
```

```python
import functools

import numpy as np
import jax
import jax.numpy as jnp
from jax import lax
from jax.experimental import pallas as pl
from jax.experimental.pallas import tpu as pltpu

F32 = jnp.float32
BF16 = jnp.bfloat16
HI = lax.Precision.HIGHEST
NT = (((1,), (1,)), ((), ()))
TN = (((0,), (0,)), ((), ()))

EPS = 1e-6
N_MOD = 6
C_CONV = 512
CONV_HALO = 32
NSA_HEADS = 8
NSA_G = 2
NSA_HPG = NSA_HEADS // NSA_G
NSA_DH = 64
NSA_WIDTH = NSA_HEADS * NSA_DH
ROPE_DIM = NSA_DH // 4
ROPE_THETA = 500000.0
L_CMP = 32
STRIDE_CMP = 16
L_SLC = 64
N_SLC_TOP = 8
WINDOW = 512
NSA_TQ = 256
NSA_TK = 128
NSA_KB = 512
NSA_VR = NSA_DH + 16
GDN_H = 4
GDN_DK = 128
GDN_DV = 128
GDN_WIDTH = GDN_H * GDN_DV
GDN_QKV = GDN_H * (2 * GDN_DK + GDN_DV)
GDN_CHUNK = 64
GDN_HALO = 8
GDN_STEP_CHUNKS = 4
GDN_SOLVE_BLOCK = 16
SM_GATE = 0
SM_BETA = 24
SM_A = 28
LANE = 128

VMEM_LIMIT = 56 * 1024 * 1024


def _dot(a, b, dims=None, hi=False):
    if dims is None:
        dims = (((a.ndim - 1,), (0,)), ((), ()))
    if hi:
        return lax.dot_general(a.astype(F32), b.astype(F32), dims, precision=HI,
                               preferred_element_type=F32)
    return lax.dot_general(a.astype(BF16), b.astype(BF16), dims, preferred_element_type=F32)


def _masked_softmax0(s, mask):
    s = jnp.where(mask, s, -1e30)
    m = jnp.max(s, axis=0, keepdims=True)
    p = jnp.where(mask, jnp.exp(s - m), 0.0)
    return p * (1.0 / jnp.maximum(jnp.sum(p, axis=0, keepdims=True), 1e-30))


def _exp_masked_bf16(sh, m, mb):
    return jnp.exp((sh - m).astype(BF16)) * mb


def _params(*sem):
    return pltpu.CompilerParams(dimension_semantics=sem, vmem_limit_bytes=VMEM_LIMIT)


def _mod_kernel(c_ref, w_ref, b_ref, o_ref):
    c = c_ref[...]
    ca = c * jax.nn.sigmoid(c)
    o_ref[0] = _dot(ca, w_ref[0], hi=True) + b_ref[0]


def _modulation(c, ada_w, ada_b):
    depth, d, n = ada_w.shape
    bsz = c.shape[0]
    tn = 1536
    return pl.pallas_call(
        _mod_kernel,
        grid=(depth, n // tn),
        in_specs=[pl.BlockSpec((bsz, d), lambda l, j: (0, 0)),
                  pl.BlockSpec((1, d, tn), lambda l, j: (l, 0, j)),
                  pl.BlockSpec((1, 1, tn), lambda l, j: (l, 0, j))],
        out_specs=pl.BlockSpec((1, bsz, tn), lambda l, j: (l, 0, j)),
        out_shape=jax.ShapeDtypeStruct((depth, bsz, n), F32),
        compiler_params=_params("parallel", "parallel"),
        name="adaln_mod",
    )(c, ada_w, ada_b.reshape(depth, 1, n))


def _norm_mod(x, g, sh, sc):
    y = x * lax.rsqrt(jnp.mean(x * x, axis=-1, keepdims=True) + EPS)
    return (y * g) * (1.0 + sc) + sh


def _inproj_kernel(x_ref, mod_ref, g_ref, w_ref, b_ref, o_ref, h_ref):
    @pl.when(pl.program_id(1) == 0)
    def _():
        m = mod_ref[0]
        h_ref[...] = _norm_mod(x_ref[...], g_ref[...], m[0:1], m[1:2]).astype(BF16)

    o_ref[...] = jnp.dot(h_ref[...], w_ref[...], preferred_element_type=F32) + b_ref[...]


def _in_projection(x2, mod_l, g, w, b, seq, tm=1024, tn=1536):
    t, d = x2.shape
    n = w.shape[1]
    per_b = seq // tm
    return pl.pallas_call(
        _inproj_kernel,
        grid=(t // tm, n // tn),
        in_specs=[pl.BlockSpec((tm, d), lambda i, j: (i, 0)),
                  pl.BlockSpec((1, N_MOD, d), lambda i, j: (i // per_b, 0, 0)),
                  pl.BlockSpec((1, d), lambda i, j: (0, 0)),
                  pl.BlockSpec((d, tn), lambda i, j: (0, j)),
                  pl.BlockSpec((1, tn), lambda i, j: (0, j))],
        out_specs=pl.BlockSpec((tm, tn), lambda i, j: (i, j)),
        out_shape=jax.ShapeDtypeStruct((t, n), F32),
        scratch_shapes=[pltpu.VMEM((tm, d), BF16)],
        compiler_params=_params("parallel", "arbitrary"),
        name="in_proj",
    )(x2, mod_l, g, w, b)


def _conv_kernel(u_ref, w_ref, b_ref, lg_ref, lb_ref, o_ref, buf_ref, *, ts, k_len, rc):
    c = C_CONV

    @pl.when(pl.program_id(1) == 0)
    def _():
        buf_ref[0:CONV_HALO, :] = jnp.zeros((CONV_HALO, c), F32)

    @pl.when(pl.program_id(1) > 0)
    def _():
        buf_ref[0:CONV_HALO, :] = buf_ref[ts:ts + CONV_HALO, :]

    u = u_ref[...]
    buf_ref[CONV_HALO:CONV_HALO + ts, :] = u[:, :c] * jax.nn.sigmoid(u[:, c:])
    base = CONV_HALO - (k_len - 1)

    def chunk(ci, carry):
        r0 = pl.multiple_of(ci * rc, rc)
        win = buf_ref[pl.ds(r0, rc + CONV_HALO), :]
        acc = jnp.zeros((rc // 8, 8, c), F32)
        for ph in range(8):
            taps = [k for k in range(k_len) if (base + k) % 8 == ph]
            if not taps:
                continue
            span = max(base + k for k in taps) - ph + rc
            shifted = win[ph:ph + span, :]
            for k in taps:
                off = base + k - ph
                acc = acc + w_ref[k][None] * shifted[off:off + rc, :].reshape(rc // 8, 8, c)
        y = acc.reshape(rc, c) + b_ref[...]
        mu = jnp.mean(y, axis=-1, keepdims=True)
        var = jnp.mean(jnp.square(y - mu), axis=-1, keepdims=True)
        y = (y - mu) * lax.rsqrt(var + EPS) * lg_ref[...] + lb_ref[...]
        o_ref[pl.ds(r0, rc), :] = (y * jax.nn.sigmoid(y)).astype(o_ref.dtype)
        return carry

    lax.fori_loop(0, ts // rc, chunk, 0)


def _conformer_conv(proj, col_blk, bsz, seq, dw_w, dw_b, ln_g, ln_b, ts=512, rc=32):
    t = bsz * seq
    k_len = dw_w.shape[0]
    per_b = seq // ts
    kern = functools.partial(_conv_kernel, ts=ts, k_len=k_len, rc=rc)
    vec = lambda: pl.BlockSpec((1, C_CONV), lambda b, s: (0, 0))
    return pl.pallas_call(
        kern,
        grid=(bsz, per_b),
        in_specs=[pl.BlockSpec((ts, 2 * C_CONV), lambda b, s: (b * per_b + s, col_blk)),
                  pl.BlockSpec((k_len, 8, C_CONV), lambda b, s: (0, 0, 0)),
                  vec(), vec(), vec()],
        out_specs=pl.BlockSpec((ts, C_CONV), lambda b, s: (b * per_b + s, 0)),
        out_shape=jax.ShapeDtypeStruct((t, C_CONV), BF16),
        scratch_shapes=[pltpu.VMEM((CONV_HALO + ts, C_CONV), F32)],
        compiler_params=_params("parallel", "arbitrary"),
        name="conformer_conv",
    )(proj, jnp.broadcast_to(dw_w[:, None, :], (k_len, 8, C_CONV)),
      dw_b.reshape(1, -1), ln_g.reshape(1, -1), ln_b.reshape(1, -1))


def _rope_table_kernel(pos_ref, inv_ref, sign_ref, cos_ref, sin_ref):
    ang = pos_ref[0].astype(F32) * inv_ref[...]
    cos_ref[0] = jnp.cos(ang)
    sin_ref[0] = jnp.sin(ang) * sign_ref[...]


def _rope_tables(positions):
    bsz, seq = positions.shape
    half = ROPE_DIM // 2
    inv = ROPE_THETA ** (-jnp.arange(half, dtype=F32) * 2.0 / ROPE_DIM)
    d = np.arange(LANE) % NSA_DH
    inv_lane = jnp.where(jnp.asarray(d < ROPE_DIM), inv[d % half], 0.0).reshape(1, LANE)
    sign_lane = jnp.asarray(np.where(d < half, -1.0, np.where(d < ROPE_DIM, 1.0, 0.0)),
                            dtype=F32).reshape(1, LANE)
    out = jax.ShapeDtypeStruct((bsz, seq, LANE), F32)
    return pl.pallas_call(
        _rope_table_kernel,
        grid=(bsz,),
        in_specs=[pl.BlockSpec((1, seq, 1), lambda b: (b, 0, 0)),
                  pl.BlockSpec((1, LANE), lambda b: (0, 0)),
                  pl.BlockSpec((1, LANE), lambda b: (0, 0))],
        out_specs=[pl.BlockSpec((1, seq, LANE), lambda b: (b, 0, 0))] * 2,
        out_shape=[out, out],
        compiler_params=_params("parallel"),
        name="rope_tables",
    )(positions.reshape(bsz, seq, 1), inv_lane, sign_lane)


def _rope_apply(x, cos_t, sin_t, width):
    lane = lax.broadcasted_iota(jnp.int32, (1, width), 1)
    lo = (lane % NSA_DH) < (ROPE_DIM // 2)
    partner = jnp.where(lo, pltpu.roll(x, width - ROPE_DIM // 2, 1), pltpu.roll(x, ROPE_DIM // 2, 1))
    return x * cos_t + partner * sin_t


def _nsa_kernel(q_ref, kv_ref, sm_ref, cos_ref, sin_ref, pe_ref, wc_ref, ovl_ref, gsel_ref,
                o_ref,
                qt_ref, csrc_ref, ccat_ref, kc_ref, vct_ref, kslc_ref, vslct_ref, kwin_ref, vwint_ref,
                sel_ref, *, seq):
    tq, tk, kb, dh, hpg, vr = NSA_TQ, NSA_TK, NSA_KB, NSA_DH, NSA_HPG, NSA_VR
    nc = seq // STRIDE_CMP
    ns = seq // L_SLC
    nwb = WINDOW // tk
    band = WINDOW + tq
    scale = dh ** -0.5
    qw = hpg * dh
    n_top = float(min(N_SLC_TOP, ns))
    nsb = kb // L_SLC
    lane = lax.broadcasted_iota(jnp.int32, (1, LANE), 1)

    csrc_ref[seq:seq + L_CMP, :] = jnp.zeros((L_CMP, LANE), F32)
    kwin_ref[0:WINDOW, :] = jnp.zeros((WINDOW, dh), BF16)
    vwint_ref[0:nwb] = jnp.zeros((nwb, vr, tk), BF16)
    ones_rows = jnp.where(lax.broadcasted_iota(jnp.int32, (vr - dh, tk), 0) == 0, 1.0, 0.0).astype(BF16)

    def prep(blk, carry):
        r0 = pl.multiple_of(blk * tk, tk)
        cos_t = cos_ref[0, pl.ds(r0, tk), :]
        sin_t = sin_ref[0, pl.ds(r0, tk), :]
        cq = jnp.concatenate([cos_t] * (qw // LANE), axis=1)
        sq = jnp.concatenate([sin_t] * (qw // LANE), axis=1)
        qr = _rope_apply(q_ref[pl.ds(r0, tk), :], cq, sq, qw) * scale
        qt_ref[blk] = qr.T.astype(BF16)
        ck = jnp.where(lane < dh, cos_t, 1.0)
        sk = jnp.where(lane < dh, sin_t, 0.0)
        kv = kv_ref[pl.ds(r0, tk), :]
        csrc_ref[pl.ds(r0, tk), :] = _rope_apply(kv[:, 0:LANE], ck, sk, LANE)
        kv_s = _rope_apply(kv[:, LANE:2 * LANE], ck, sk, LANE)
        kslc_ref[pl.ds(r0, tk), :] = kv_s[:, :dh].astype(BF16)
        vslct_ref[blk] = jnp.concatenate([kv_s.T[dh:, :].astype(BF16), ones_rows], axis=0)
        kv_w = _rope_apply(kv[:, 2 * LANE:3 * LANE], ck, sk, LANE)
        kwin_ref[pl.ds(pl.multiple_of(WINDOW + r0, tk), tk), :] = kv_w[:, :dh].astype(BF16)
        vwint_ref[nwb + blk] = jnp.concatenate([kv_w.T[dh:, :].astype(BF16), ones_rows], axis=0)
        return carry

    lax.fori_loop(0, seq // tk, prep, 0)

    for l in range(L_CMP):
        ccat_ref[:, l * LANE:(l + 1) * LANE] = (
            csrc_ref[pl.ds(l, nc, stride=STRIDE_CMP), :] + pe_ref[l:l + 1, :])
    kvc = _dot(ccat_ref[...], wc_ref[...])
    kc_ref[...] = kvc[:, :dh].astype(BF16)
    vct_ref[...] = kvc.T[dh:, :].astype(BF16)

    n_sub = lax.broadcasted_iota(jnp.int32, (nc, 1), 0)
    j_sub = lax.broadcasted_iota(jnp.int32, (ns, 1), 0)
    k_sub = lax.broadcasted_iota(jnp.int32, (kb, 1), 0)
    c_sub = lax.broadcasted_iota(jnp.int32, (band, 1), 0)
    q_lane = lax.broadcasted_iota(jnp.int32, (1, tq), 1)
    win_static = (c_sub - WINDOW <= q_lane) & (c_sub > q_lane)
    heads = [slice(h * tq, (h + 1) * tq) for h in range(hpg)]

    def q_tile(i, carry):
        r0 = pl.multiple_of(i * tq, tq)
        b0 = i * (tq // tk)
        qblk = [qt_ref[b0 + jb] for jb in range(tq // tk)]
        qt4 = jnp.concatenate([qb[h * dh:(h + 1) * dh, :] for h in range(hpg) for qb in qblk],
                              axis=1)
        t_lane = r0 + q_lane

        s = _dot(kc_ref[...], qt4)
        cmask = (n_sub * STRIDE_CMP + (L_CMP - 1)) <= t_lane
        ps = [_masked_softmax0(s[:, sl], cmask) for sl in heads]
        o_cmp = _dot(vct_ref[...], jnp.concatenate([p.astype(BF16) for p in ps], axis=1))
        psum = ps[0]
        for p in ps[1:]:
            psum = psum + p

        imp = _dot(ovl_ref[...], psum, hi=True)
        cur = t_lane // L_SLC
        causal_blk = j_sub <= cur
        forced = (j_sub == 0) | (j_sub == cur) | (j_sub == cur - 1)
        impv = jnp.where(forced, 1e9, jnp.where(causal_blk, imp, -1e9))
        rank = jnp.zeros((ns, tq), F32)
        for b in range(ns):
            row = impv[b:b + 1, :]
            beats = (row > impv) | ((row == impv) & (b < j_sub))
            rank = rank + jnp.where(beats, 1.0, 0.0)
        sel = jnp.where((rank < n_top) & causal_blk, 1.0, 0.0)
        sel_ref[...] = sel

        s = _dot(kwin_ref[pl.ds(r0, band), :], qt4)
        wmask = win_static & (c_sub >= WINDOW - r0)
        wmb = jnp.where(wmask, 1.0, 0.0).astype(BF16)
        ps = []
        for sl in heads:
            sh = jnp.where(wmask, s[:, sl], -1e30)
            ps.append(_exp_masked_bf16(sh, jnp.max(sh, axis=0, keepdims=True), wmb))
        p4 = jnp.concatenate(ps, axis=1)
        ow = _dot(vwint_ref[b0], p4[0:tk, :])
        for jb in range(1, band // tk):
            ow = ow + _dot(vwint_ref[b0 + jb], p4[jb * tk:(jb + 1) * tk, :])
        o_win = ow[0:dh] * (1.0 / jnp.maximum(ow[dh:dh + 1], 1e-30))

        gt = _dot(gsel_ref[0], jax.nn.sigmoid(sm_ref[pl.ds(r0, tq), :]), NT, hi=True)
        gate = lambda br: jnp.concatenate(
            [jnp.broadcast_to(gt[br * hpg + h:br * hpg + h + 1], (dh, tq)) for h in range(hpg)], axis=1)
        o_cw = gate(0) * o_cmp + gate(2) * o_win
        g_slc = gate(1)

        def kv_step(j, st):
            m, acc = st
            k0 = pl.multiple_of(j * kb, kb)
            s = _dot(kslc_ref[pl.ds(k0, kb), :], qt4)
            sel_rows = sel_ref[pl.ds(pl.multiple_of(j * nsb, nsb), nsb), :]
            selm = jnp.concatenate([jnp.broadcast_to(sel_rows[b:b + 1, :], (L_SLC, tq)) for b in range(nsb)], axis=0)
            msk = (selm > 0.5) & (k0 + k_sub <= t_lane)
            mb = jnp.where(msk, 1.0, 0.0).astype(BF16)
            ms, als, ps = [], [], []
            for sl in heads:
                sh = jnp.where(msk, s[:, sl], -1e30)
                mn = jnp.maximum(m[:, sl], jnp.max(sh, axis=0, keepdims=True))
                ms.append(mn)
                als.append(jnp.exp(m[:, sl] - mn))
                ps.append(_exp_masked_bf16(sh, mn, mb))
            p4 = jnp.concatenate(ps, axis=1)
            pv = _dot(vslct_ref[j * (kb // tk)], p4[0:tk, :])
            for hf in range(1, kb // tk):
                pv = pv + _dot(vslct_ref[j * (kb // tk) + hf], p4[hf * tk:(hf + 1) * tk, :])
            return jnp.concatenate(ms, axis=1), jnp.concatenate(als, axis=1) * acc + pv

        init = (jnp.full((1, hpg * tq), -1e30, F32), jnp.zeros((vr, hpg * tq), F32))
        _, acc_s = lax.fori_loop(0, (r0 + tq + kb - 1) // kb, kv_step, init)
        o_slc = acc_s[0:dh] * (1.0 / jnp.maximum(acc_s[dh:dh + 1], 1e-30))

        out_t = o_cw + g_slc * o_slc
        out_t = jnp.concatenate([out_t[:, sl] for sl in heads], axis=0)
        o_ref[pl.ds(r0, tq), :] = out_t.T.astype(o_ref.dtype)
        return carry

    lax.fori_loop(0, seq // tq, q_tile, 0)


def _nsa_constants(seq):
    nc = seq // STRIDE_CMP
    ns = seq // L_SLC
    cs = np.arange(nc) * STRIDE_CMP
    js = np.arange(ns) * L_SLC
    ovl_t = ((cs[None, :] < js[:, None] + L_SLC) & (cs[None, :] + L_CMP > js[:, None])).astype(np.float32)
    gsel = np.zeros((NSA_G, 16, LANE), np.float32)
    for g in range(NSA_G):
        for h in range(NSA_HPG):
            for j in range(3):
                gsel[g, j * NSA_HPG + h, SM_GATE + (g * NSA_HPG + h) * 3 + j] = 1.0
    return jnp.asarray(ovl_t), jnp.asarray(gsel)


def _nsa(proj, q_blk, kv_blk, sm_blk, bsz, seq, cos_t, sin_t, pe_k, pe_v, wk_cmp, wv_cmp, consts):
    t = bsz * seq
    dh = NSA_DH
    qw = NSA_HPG * dh
    tq, tk = NSA_TQ, NSA_TK
    nc = seq // STRIDE_CMP
    ns = seq // L_SLC
    ovl_t, gsel = consts
    pe = jnp.concatenate([pe_k, pe_v], axis=1)
    zero = jnp.zeros((L_CMP, dh, dh), F32)
    wc = jnp.concatenate([jnp.concatenate([wk_cmp, zero], axis=2),
                          jnp.concatenate([zero, wv_cmp], axis=2)], axis=1)
    wc = wc.reshape(L_CMP * LANE, LANE).astype(BF16)
    kern = functools.partial(_nsa_kernel, seq=seq)
    full = lambda shape: pl.BlockSpec(shape, lambda b, g: tuple(0 for _ in shape))
    return pl.pallas_call(
        kern,
        grid=(bsz, NSA_G),
        in_specs=[pl.BlockSpec((seq, qw), lambda b, g: (b, q_blk + g)),
                  pl.BlockSpec((seq, 3 * LANE), lambda b, g: (b, kv_blk + g)),
                  pl.BlockSpec((seq, LANE), lambda b, g: (b, sm_blk)),
                  pl.BlockSpec((1, seq, LANE), lambda b, g: (b, 0, 0)),
                  pl.BlockSpec((1, seq, LANE), lambda b, g: (b, 0, 0)),
                  full((L_CMP, LANE)),
                  full((L_CMP * LANE, LANE)),
                  full((ns, nc)),
                  pl.BlockSpec((1, 16, LANE), lambda b, g: (g, 0, 0))],
        out_specs=pl.BlockSpec((seq, qw), lambda b, g: (b, g)),
        out_shape=jax.ShapeDtypeStruct((t, NSA_G * qw), BF16),
        scratch_shapes=[pltpu.VMEM((seq // tk, qw, tk), BF16),
                        pltpu.VMEM((seq + L_CMP, LANE), F32),
                        pltpu.VMEM((nc, L_CMP * LANE), F32),
                        pltpu.VMEM((nc, dh), BF16),
                        pltpu.VMEM((dh, nc), BF16),
                        pltpu.VMEM((seq, dh), BF16),
                        pltpu.VMEM((seq // tk, NSA_VR, tk), BF16),
                        pltpu.VMEM((WINDOW + seq, dh), BF16),
                        pltpu.VMEM(((WINDOW + seq) // tk, NSA_VR, tk), BF16),
                        pltpu.VMEM((ns, tq), F32)],
        compiler_params=_params("parallel", "parallel"),
        name="nsa",
    )(proj, proj, proj, cos_t, sin_t, pe, wc, ovl_t, gsel)


def _dot3(a, b):
    a_hi = a.astype(BF16)
    a_lo = (a - a_hi.astype(F32)).astype(BF16)
    b_hi = b.astype(BF16)
    b_lo = (b - b_hi.astype(F32)).astype(BF16)
    d = lambda x, y: jnp.dot(x, y, preferred_element_type=F32)
    return d(a_hi, b_hi) + (d(a_lo, b_hi) + d(a_hi, b_lo))


def _unit_lower_solve(lmats, rhss):
    c, width = rhss[0].shape
    bs = GDN_SOLVE_BLOCK
    done = [[] for _ in rhss]
    for b in range(c // bs):
        r0 = b * bs
        xbs = []
        for lmat, rhs, dn in zip(lmats, rhss, done):
            xb = rhs[r0:r0 + bs]
            if b:
                partial = jnp.concatenate(dn + [jnp.zeros((c - r0, width), F32)], axis=0)
                xb = xb - _dot3(lmat[r0:r0 + bs, :], partial)
            xbs.append(xb)
        for lmat, xb, dn in zip(lmats, xbs, done):
            lb = lmat[r0:r0 + bs, r0:r0 + bs]
            groups = [xb[8 * g:8 * g + 8] for g in range(bs // 8)]
            for j in range(bs - 1):
                row = groups[j // 8][j % 8:j % 8 + 1, :]
                for g in range(j // 8, bs // 8):
                    if 8 * g + 7 > j:
                        groups[g] = groups[g] - lb[8 * g:8 * g + 8, j:j + 1] * row
            dn.append(jnp.concatenate(groups, axis=0))
    return [jnp.concatenate(dn, axis=0) for dn in done]


def _gdn_kernel(qkv_ref, z_ref, sm_ref, cw_ref, alog_ref, dtb_ref, ng_ref, o_ref, buf_ref, xc_ref, st_ref,
                *, k_len):
    c = GDN_CHUNK
    rows = c * GDN_STEP_CHUNKS
    dk, dv = GDN_DK, GDN_DV

    @pl.when(pl.program_id(1) == 0)
    def _():
        buf_ref[0:GDN_HALO, :] = jnp.zeros((GDN_HALO, GDN_QKV), F32)
        st_ref[...] = jnp.zeros(st_ref.shape, F32)

    x = qkv_ref[...]
    buf_ref[GDN_HALO:GDN_HALO + rows, :] = x
    base = GDN_HALO - (k_len - 1)
    acc = jnp.zeros((rows, GDN_QKV), F32)
    for k in range(k_len):
        acc = acc + cw_ref[k:k + 1, :] * buf_ref[base + k:base + k + rows, :]
    buf_ref[0:GDN_HALO, :] = x[rows - GDN_HALO:rows, :]
    xc_ref[...] = acc * jax.nn.sigmoid(acc)

    sm = sm_ref[...]
    beta_all = jax.nn.sigmoid(sm)
    g_all = -jnp.exp(alog_ref[...]) * jax.nn.softplus(sm + dtb_ref[...])
    ri = lax.broadcasted_iota(jnp.int32, (rows, rows), 0)
    rj = lax.broadcasted_iota(jnp.int32, (rows, rows), 1)
    same_chunk_incl = (ri >= rj) & (ri // c == rj // c)
    gc_all = _dot(jnp.where(same_chunk_incl, 1.0, 0.0), g_all, hi=True)
    gc_rows = gc_all.T
    ii = lax.broadcasted_iota(jnp.int32, (c, c), 0)
    jj = lax.broadcasted_iota(jnp.int32, (c, c), 1)
    incl = ii >= jj
    strict = ii > jj

    pre, lmats, rhss = [], [], []
    for n in range(GDN_STEP_CHUNKS):
        rs = slice(n * c, (n + 1) * c)
        for h in range(GDN_H):
            gcol = gc_all[rs, SM_A + h:SM_A + h + 1]
            bcol = beta_all[rs, SM_BETA + h:SM_BETA + h + 1]
            grow = gc_rows[SM_A + h:SM_A + h + 1, rs]
            decay = jnp.exp(jnp.where(incl, gcol - grow, -jnp.inf))
            q = xc_ref[rs, h * dk:(h + 1) * dk]
            k = xc_ref[rs, GDN_H * dk + h * dk:GDN_H * dk + (h + 1) * dk]
            v = xc_ref[rs, 2 * GDN_H * dk + h * dv:2 * GDN_H * dk + (h + 1) * dv]
            q = q * lax.rsqrt(jnp.sum(q * q, axis=-1, keepdims=True) + EPS) * (dk ** -0.5)
            k = k * lax.rsqrt(jnp.sum(k * k, axis=-1, keepdims=True) + EPS)
            eg = jnp.exp(gcol)
            lmat = jnp.where(strict, bcol * _dot(k, k, NT) * decay, 0.0)
            qk = jnp.where(incl, _dot(q, k, NT) * decay, 0.0)
            glast = gcol[c - 1:c, :]
            lmats.append(lmat)
            rhss.append(jnp.concatenate([k * (bcol * eg), v * bcol], axis=1))
            pre.append((q * eg, k * jnp.exp(glast - gcol), qk, jnp.exp(glast)))
    sols = _unit_lower_solve(lmats, rhss)

    z = z_ref[...]
    for n in range(GDN_STEP_CHUNKS):
        rs = slice(n * c, (n + 1) * c)
        idx = range(n * GDN_H, (n + 1) * GDN_H)
        states = [st_ref[h] for h in range(GDN_H)]
        v_news = [sols[i][:, dk:] - _dot(sols[i][:, :dk], st) for i, st in zip(idx, states)]
        o_state = [_dot(pre[i][0], st) for i, st in zip(idx, states)]
        o_intra = [_dot(pre[i][2], vn) for i, vn in zip(idx, v_news)]
        s_add = [_dot(pre[i][1], vn, TN) for i, vn in zip(idx, v_news)]
        for h, i in enumerate(idx):
            st_ref[h] = pre[i][3] * states[h] + s_add[h]
            o = o_state[h] + o_intra[h]
            o = o * lax.rsqrt(jnp.mean(o * o, axis=-1, keepdims=True) + EPS) * ng_ref[...]
            zh = z[rs, h * dv:(h + 1) * dv]
            o_ref[rs, h * dv:(h + 1) * dv] = (o * (zh * jax.nn.sigmoid(zh))).astype(o_ref.dtype)


def _gdn(proj, qkv_blk, z_blk, sm_blk, bsz, seq, conv_w, a_log, dt_bias, norm_g):
    t = bsz * seq
    c = GDN_CHUNK * GDN_STEP_CHUNKS
    nch = seq // c
    k_len = conv_w.shape[0]
    alog = jnp.zeros((1, LANE), F32).at[0, SM_A:SM_A + GDN_H].set(a_log)
    dtb = jnp.zeros((1, LANE), F32).at[0, SM_A:SM_A + GDN_H].set(dt_bias)
    kern = functools.partial(_gdn_kernel, k_len=k_len)
    full = lambda shape: pl.BlockSpec(shape, lambda b, n: tuple(0 for _ in shape))
    return pl.pallas_call(
        kern,
        grid=(bsz, nch),
        in_specs=[pl.BlockSpec((c, GDN_QKV), lambda b, n: (b * nch + n, qkv_blk)),
                  pl.BlockSpec((c, GDN_WIDTH), lambda b, n: (b * nch + n, z_blk)),
                  pl.BlockSpec((c, LANE), lambda b, n: (b * nch + n, sm_blk)),
                  full((k_len, GDN_QKV)), full((1, LANE)), full((1, LANE)), full((1, GDN_DV))],
        out_specs=pl.BlockSpec((c, GDN_WIDTH), lambda b, n: (b * nch + n, 0)),
        out_shape=jax.ShapeDtypeStruct((t, GDN_WIDTH), BF16),
        scratch_shapes=[pltpu.VMEM((GDN_HALO + c, GDN_QKV), F32),
                        pltpu.VMEM((c, GDN_QKV), F32),
                        pltpu.VMEM((GDN_H, GDN_DK, GDN_DV), F32)],
        compiler_params=_params("parallel", "arbitrary"),
        name="gdn",
    )(proj, proj, proj, conv_w, alog, dtb, norm_g.reshape(1, -1))


def _merge_kernel(x_ref, mod_ref, yc_ref, yn_ref, yg_ref, gm_ref, wc_ref, wn_ref, wg_ref, wo_ref, o_ref):
    d = x_ref.shape[1]
    gm = jax.nn.sigmoid(gm_ref[...])
    merged = (gm[:, 0:d] * jnp.dot(yc_ref[...], wc_ref[...], preferred_element_type=F32)
              + gm[:, d:2 * d] * jnp.dot(yn_ref[...], wn_ref[...], preferred_element_type=F32)
              + gm[:, 2 * d:3 * d] * jnp.dot(yg_ref[...], wg_ref[...], preferred_element_type=F32))
    gt = mod_ref[0][2:3]
    o_ref[...] = x_ref[...] + gt * _dot(merged, wo_ref[...])


def _merge(x2, mod_l, y_conv, y_nsa, y_gdn, proj, gm_blk, w_c, w_n, w_g, w_o, seq, tm=512):
    t, d = x2.shape
    per_b = seq // tm
    row = lambda w: pl.BlockSpec((tm, w), lambda i: (i, 0))
    wspec = lambda w: pl.BlockSpec(w.shape, lambda i: (0, 0))
    return pl.pallas_call(
        _merge_kernel,
        grid=(t // tm,),
        in_specs=[row(d),
                  pl.BlockSpec((1, N_MOD, d), lambda i: (i // per_b, 0, 0)),
                  row(y_conv.shape[1]), row(y_nsa.shape[1]), row(y_gdn.shape[1]),
                  pl.BlockSpec((tm, 3 * d), lambda i: (i, gm_blk)),
                  wspec(w_c), wspec(w_n), wspec(w_g), wspec(w_o)],
        out_specs=row(d),
        out_shape=jax.ShapeDtypeStruct((t, d), F32),
        compiler_params=_params("parallel"),
        name="merge",
    )(x2, mod_l, y_conv, y_nsa, y_gdn, proj, w_c, w_n, w_g, w_o)


def _ffn_kernel(x_ref, mod_ref, g_ref, wg_ref, wu_ref, wo_ref, fg_ref, o_ref, h_ref, acc_ref, *, final):
    j = pl.program_id(1)

    @pl.when(j == 0)
    def _():
        m = mod_ref[0]
        h_ref[...] = _norm_mod(x_ref[...], g_ref[...], m[3:4], m[4:5]).astype(BF16)
        acc_ref[...] = jnp.zeros(acc_ref.shape, F32)

    h = h_ref[...]
    gate = jnp.dot(h, wg_ref[...], preferred_element_type=F32)
    up = jnp.dot(h, wu_ref[...], preferred_element_type=F32)
    act = (gate * jax.nn.sigmoid(gate)) * up
    acc_ref[...] += _dot(act, wo_ref[...])

    @pl.when(j == pl.num_programs(1) - 1)
    def _():
        y = x_ref[...] + mod_ref[0][5:6] * acc_ref[...]
        if final:
            y = y * lax.rsqrt(jnp.mean(y * y, axis=-1, keepdims=True) + EPS) * fg_ref[...]
        o_ref[...] = y


def _ffn(x2, mod_l, g, w_in, w_out, final_g, seq, final, tm=512, th=1408):
    t, d = x2.shape
    hid = w_out.shape[0]
    per_b = seq // tm
    nh = hid // th
    kern = functools.partial(_ffn_kernel, final=final)
    return pl.pallas_call(
        kern,
        grid=(t // tm, nh),
        in_specs=[pl.BlockSpec((tm, d), lambda i, j: (i, 0)),
                  pl.BlockSpec((1, N_MOD, d), lambda i, j: (i // per_b, 0, 0)),
                  pl.BlockSpec((1, d), lambda i, j: (0, 0)),
                  pl.BlockSpec((d, th), lambda i, j: (0, j)),
                  pl.BlockSpec((d, th), lambda i, j: (0, nh + j)),
                  pl.BlockSpec((th, d), lambda i, j: (j, 0)),
                  pl.BlockSpec((1, d), lambda i, j: (0, 0))],
        out_specs=pl.BlockSpec((tm, d), lambda i, j: (i, 0)),
        out_shape=jax.ShapeDtypeStruct((t, d), F32),
        scratch_shapes=[pltpu.VMEM((tm, d), BF16), pltpu.VMEM((tm, d), F32)],
        compiler_params=_params("parallel", "arbitrary"),
        name="ffn",
    )(x2, mod_l, g, w_in, w_in, w_out, final_g)


def _in_layout(d):
    segs = {}
    off = 0
    for name, width in (("gm", 3 * d), ("conv", 2 * C_CONV), ("q", NSA_WIDTH), ("qkv", GDN_QKV),
                        ("z", GDN_WIDTH), ("small", LANE), ("pad", LANE),
                        ("kv", 6 * NSA_G * NSA_DH)):
        segs[name] = off
        off += width
    return segs, off


def _rearrange_in_proj(w_in, b_in, d):
    splits = (2 * C_CONV, NSA_WIDTH, 6 * NSA_G * NSA_DH, 3 * NSA_HEADS, GDN_QKV, GDN_WIDTH, GDN_H, GDN_H, 3 * d)
    offs = np.concatenate([[0], np.cumsum(splits)])
    names = ("conv", "q", "kv", "gnsa", "qkv", "z", "beta", "a", "gm")
    segs, total = _in_layout(d)

    def arrange(a):
        src = {n: a[..., int(offs[i]):int(offs[i + 1])] for i, n in enumerate(names)}
        lead = a.shape[:-1]
        kv = jnp.swapaxes(src["kv"].reshape(lead + (6, NSA_G, NSA_DH)), -3, -2).reshape(lead + (-1,))
        small_pad = jnp.zeros(lead + (LANE - 3 * NSA_HEADS - 2 * GDN_H,), a.dtype)
        pad = jnp.zeros(lead + (LANE,), a.dtype)
        out = jnp.concatenate([src["gm"], src["conv"], src["q"], src["qkv"], src["z"],
                               src["gnsa"], src["beta"], src["a"], small_pad, pad, kv], axis=-1)
        assert out.shape[-1] == total
        return out

    return arrange(w_in).astype(BF16), arrange(b_in)[:, None, :], segs


def kernel(x, c, positions, ada_w, ada_b, norm_mix_g, norm_ffn_g, w_in, b_in, conv_dw_w, conv_dw_b, conv_ln_g, conv_ln_b, nsa_pe_k, nsa_pe_v, nsa_wk_cmp, nsa_wv_cmp, gdn_conv_w, gdn_a_log, gdn_dt_bias, gdn_norm_g, w_up_conv, w_up_nsa, w_up_gdn, w_o, ffn_w_in, ffn_w_out, final_norm_g):
    bsz, seq, d = x.shape
    depth = ada_w.shape[0]
    t = bsz * seq

    mod = _modulation(c, ada_w, ada_b).reshape(depth, bsz, N_MOD, d)
    cos_t, sin_t = _rope_tables(positions)
    nsa_consts = _nsa_constants(seq)
    w_in_r, b_in_r, segs = _rearrange_in_proj(w_in, b_in, d)
    blk = lambda name, width: segs[name] // width
    qw = NSA_HPG * NSA_DH

    x2 = x.reshape(t, d)
    for l in range(depth):
        proj = _in_projection(x2, mod[l], norm_mix_g[l].reshape(1, d), w_in_r[l], b_in_r[l], seq)
        y_conv = _conformer_conv(proj, blk("conv", 2 * C_CONV), bsz, seq,
                                 conv_dw_w[l], conv_dw_b[l], conv_ln_g[l], conv_ln_b[l])
        y_nsa = _nsa(proj, blk("q", qw), blk("kv", 3 * LANE), blk("small", LANE), bsz, seq, cos_t, sin_t,
                     nsa_pe_k[l], nsa_pe_v[l], nsa_wk_cmp[l], nsa_wv_cmp[l], nsa_consts)
        y_gdn = _gdn(proj, blk("qkv", GDN_QKV), blk("z", GDN_WIDTH), blk("small", LANE), bsz, seq,
                     gdn_conv_w[l], gdn_a_log[l], gdn_dt_bias[l], gdn_norm_g[l])
        x2 = _merge(x2, mod[l], y_conv, y_nsa, y_gdn, proj, blk("gm", 3 * d),
                    w_up_conv[l].astype(BF16), w_up_nsa[l].astype(BF16), w_up_gdn[l].astype(BF16),
                    w_o[l].astype(BF16), seq)
        x2 = _ffn(x2, mod[l], norm_ffn_g[l].reshape(1, d), ffn_w_in[l].astype(BF16),
                  ffn_w_out[l].astype(BF16), final_norm_g.reshape(1, d), seq, final=(l == depth - 1))
    return x2.reshape(bsz, seq, d)
```

```python
import functools

import numpy as np
import jax
import jax.numpy as jnp
from jax import lax
from jax.experimental import pallas as pl
from jax.experimental.pallas import tpu as pltpu

F32 = jnp.float32
BF16 = jnp.bfloat16
HI = lax.Precision.HIGHEST
NT = (((1,), (1,)), ((), ()))
TN = (((0,), (0,)), ((), ()))

EPS = 1e-6
LOG2_E = 1.4426950408889634
N_MOD = 6
C_CONV = 512
CONV_HALO = 32
NSA_HEADS = 8
NSA_G = 2
NSA_HPG = NSA_HEADS // NSA_G
NSA_DH = 64
NSA_WIDTH = NSA_HEADS * NSA_DH
ROPE_DIM = NSA_DH // 4
ROPE_THETA = 500000.0
L_CMP = 32
STRIDE_CMP = 16
L_SLC = 64
N_SLC_TOP = 8
WINDOW = 512
NSA_TQ = 512
NSA_TK = 128
NSA_KB = 512
NSA_VR = NSA_DH + 16
GDN_H = 4
GDN_DK = 128
GDN_DV = 128
GDN_WIDTH = GDN_H * GDN_DV
GDN_QKV = GDN_H * (2 * GDN_DK + GDN_DV)
GDN_CHUNK = 64
GDN_HALO = 8
GDN_STEP_CHUNKS = 4
GDN_SOLVE_BLOCK = 16
SM_GATE = 0
SM_BETA = 24
SM_A = 28
LANE = 128

VMEM_LIMIT = 56 * 1024 * 1024


def _dot(a, b, dims=None, hi=False):
    if dims is None:
        dims = (((a.ndim - 1,), (0,)), ((), ()))
    if hi:
        return lax.dot_general(a.astype(F32), b.astype(F32), dims, precision=HI,
                               preferred_element_type=F32)
    return lax.dot_general(a.astype(BF16), b.astype(BF16), dims, preferred_element_type=F32)


def _masked_softmax0(s, mask):
    s = jnp.where(mask, s, -1e30)
    m = jnp.max(s, axis=0, keepdims=True)
    p = jnp.where(mask, jnp.exp2(s - m), 0.0)
    return p * (1.0 / jnp.maximum(jnp.sum(p, axis=0, keepdims=True), 1e-30))


def _exp_masked_bf16(sh, m, mb):
    return jnp.exp2((sh - m).astype(BF16)) * mb


def _params(*sem):
    return pltpu.CompilerParams(dimension_semantics=sem, vmem_limit_bytes=VMEM_LIMIT)


def _mod_kernel(c_ref, w_ref, b_ref, o_ref):
    c = c_ref[...]
    ca = c * jax.nn.sigmoid(c)
    o_ref[0] = _dot(ca, w_ref[0], hi=True) + b_ref[0]


def _modulation(c, ada_w, ada_b):
    depth, d, n = ada_w.shape
    bsz = c.shape[0]
    tn = 1536
    return pl.pallas_call(
        _mod_kernel,
        grid=(depth, n // tn),
        in_specs=[pl.BlockSpec((bsz, d), lambda l, j: (0, 0)),
                  pl.BlockSpec((1, d, tn), lambda l, j: (l, 0, j)),
                  pl.BlockSpec((1, 1, tn), lambda l, j: (l, 0, j))],
        out_specs=pl.BlockSpec((1, bsz, tn), lambda l, j: (l, 0, j)),
        out_shape=jax.ShapeDtypeStruct((depth, bsz, n), F32),
        compiler_params=_params("parallel", "parallel"),
        name="adaln_mod",
    )(c, ada_w, ada_b.reshape(depth, 1, n))


def _norm_mod(x, g, sh, sc):
    y = x * lax.rsqrt(jnp.mean(x * x, axis=-1, keepdims=True) + EPS)
    return (y * g) * (1.0 + sc) + sh


def _inproj_kernel(x_ref, mod_ref, g_ref, w_ref, b_ref, o_ref, h_ref):
    @pl.when(pl.program_id(1) == 0)
    def _():
        m = mod_ref[0]
        h_ref[...] = _norm_mod(x_ref[...], g_ref[...], m[0:1], m[1:2]).astype(BF16)

    o_ref[...] = jnp.dot(h_ref[...], w_ref[...], preferred_element_type=F32) + b_ref[...]


def _in_projection(x2, mod_l, g, w, b, seq, tm=1024, tn=1536):
    t, d = x2.shape
    n = w.shape[1]
    per_b = seq // tm
    return pl.pallas_call(
        _inproj_kernel,
        grid=(t // tm, n // tn),
        in_specs=[pl.BlockSpec((tm, d), lambda i, j: (i, 0)),
                  pl.BlockSpec((1, N_MOD, d), lambda i, j: (i // per_b, 0, 0)),
                  pl.BlockSpec((1, d), lambda i, j: (0, 0)),
                  pl.BlockSpec((d, tn), lambda i, j: (0, j)),
                  pl.BlockSpec((1, tn), lambda i, j: (0, j))],
        out_specs=pl.BlockSpec((tm, tn), lambda i, j: (i, j)),
        out_shape=jax.ShapeDtypeStruct((t, n), F32),
        scratch_shapes=[pltpu.VMEM((tm, d), BF16)],
        compiler_params=_params("parallel", "arbitrary"),
        name="in_proj",
    )(x2, mod_l, g, w, b)


def _conv_kernel(u_ref, w_ref, b_ref, lg_ref, lb_ref, o_ref, buf_ref, *, ts, k_len, rc):
    c = C_CONV

    @pl.when(pl.program_id(1) == 0)
    def _():
        buf_ref[0:CONV_HALO, :] = jnp.zeros((CONV_HALO, c), F32)

    @pl.when(pl.program_id(1) > 0)
    def _():
        buf_ref[0:CONV_HALO, :] = buf_ref[ts:ts + CONV_HALO, :]

    u = u_ref[...]
    buf_ref[CONV_HALO:CONV_HALO + ts, :] = u[:, :c] * jax.nn.sigmoid(u[:, c:])
    base = CONV_HALO - (k_len - 1)

    def chunk(ci, carry):
        r0 = pl.multiple_of(ci * rc, rc)
        win = buf_ref[pl.ds(r0, rc + CONV_HALO), :]
        acc = jnp.zeros((rc // 8, 8, c), F32)
        for ph in range(8):
            taps = [k for k in range(k_len) if (base + k) % 8 == ph]
            if not taps:
                continue
            span = max(base + k for k in taps) - ph + rc
            shifted = win[ph:ph + span, :]
            for k in taps:
                off = base + k - ph
                acc = acc + w_ref[k][None] * shifted[off:off + rc, :].reshape(rc // 8, 8, c)
        y = acc.reshape(rc, c) + b_ref[...]
        mu = jnp.mean(y, axis=-1, keepdims=True)
        var = jnp.mean(jnp.square(y - mu), axis=-1, keepdims=True)
        y = (y - mu) * lax.rsqrt(var + EPS) * lg_ref[...] + lb_ref[...]
        o_ref[pl.ds(r0, rc), :] = (y * jax.nn.sigmoid(y)).astype(o_ref.dtype)
        return carry

    lax.fori_loop(0, ts // rc, chunk, 0)


def _conformer_conv(proj, col_blk, bsz, seq, dw_w, dw_b, ln_g, ln_b, ts=512, rc=32):
    t = bsz * seq
    k_len = dw_w.shape[0]
    per_b = seq // ts
    kern = functools.partial(_conv_kernel, ts=ts, k_len=k_len, rc=rc)
    vec = lambda: pl.BlockSpec((1, C_CONV), lambda b, s: (0, 0))
    return pl.pallas_call(
        kern,
        grid=(bsz, per_b),
        in_specs=[pl.BlockSpec((ts, 2 * C_CONV), lambda b, s: (b * per_b + s, col_blk)),
                  pl.BlockSpec((k_len, 8, C_CONV), lambda b, s: (0, 0, 0)),
                  vec(), vec(), vec()],
        out_specs=pl.BlockSpec((ts, C_CONV), lambda b, s: (b * per_b + s, 0)),
        out_shape=jax.ShapeDtypeStruct((t, C_CONV), BF16),
        scratch_shapes=[pltpu.VMEM((CONV_HALO + ts, C_CONV), F32)],
        compiler_params=_params("parallel", "arbitrary"),
        name="conformer_conv",
    )(proj, jnp.broadcast_to(dw_w[:, None, :], (k_len, 8, C_CONV)),
      dw_b.reshape(1, -1), ln_g.reshape(1, -1), ln_b.reshape(1, -1))


def _rope_table_kernel(pos_ref, inv_ref, sign_ref, cos_ref, sin_ref):
    ang = pos_ref[0].astype(F32) * inv_ref[...]
    cos_ref[0] = jnp.cos(ang)
    sin_ref[0] = jnp.sin(ang) * sign_ref[...]


def _rope_tables(positions):
    bsz, seq = positions.shape
    half = ROPE_DIM // 2
    inv = ROPE_THETA ** (-jnp.arange(half, dtype=F32) * 2.0 / ROPE_DIM)
    d = np.arange(LANE) % NSA_DH
    inv_lane = jnp.where(jnp.asarray(d < ROPE_DIM), inv[d % half], 0.0).reshape(1, LANE)
    sign_lane = jnp.asarray(np.where(d < half, -1.0, np.where(d < ROPE_DIM, 1.0, 0.0)),
                            dtype=F32).reshape(1, LANE)
    out = jax.ShapeDtypeStruct((bsz, seq, LANE), F32)
    return pl.pallas_call(
        _rope_table_kernel,
        grid=(bsz,),
        in_specs=[pl.BlockSpec((1, seq, 1), lambda b: (b, 0, 0)),
                  pl.BlockSpec((1, LANE), lambda b: (0, 0)),
                  pl.BlockSpec((1, LANE), lambda b: (0, 0))],
        out_specs=[pl.BlockSpec((1, seq, LANE), lambda b: (b, 0, 0))] * 2,
        out_shape=[out, out],
        compiler_params=_params("parallel"),
        name="rope_tables",
    )(positions.reshape(bsz, seq, 1), inv_lane, sign_lane)


def _rope_apply(x, cos_t, sin_t, perm):
    partner = jnp.dot(x.astype(BF16), perm, preferred_element_type=F32)
    return x * cos_t + partner * sin_t


def _nsa_kernel(q_ref, kv_ref, sm_ref, cos_ref, sin_ref, pe_ref, wc_ref, ovl_ref, gsel_ref, perm_ref,
                o_ref,
                qt_ref, csrc_ref, ccat_ref, kc_ref, vct_ref, kslc_ref, vslct_ref, kwin_ref, vwint_ref,
                sel_ref, *, seq):
    tq, tk, kb, dh, hpg, vr = NSA_TQ, NSA_TK, NSA_KB, NSA_DH, NSA_HPG, NSA_VR
    nc = seq // STRIDE_CMP
    ns = seq // L_SLC
    nwb = WINDOW // tk
    band = WINDOW + tq
    scale = dh ** -0.5 * LOG2_E
    qw = hpg * dh
    n_top = float(min(N_SLC_TOP, ns))
    nsb = kb // L_SLC
    lane = lax.broadcasted_iota(jnp.int32, (1, LANE), 1)

    csrc_ref[seq:seq + L_CMP, :] = jnp.zeros((L_CMP, LANE), F32)
    kwin_ref[0:WINDOW, :] = jnp.zeros((WINDOW, dh), BF16)
    vwint_ref[0:nwb] = jnp.zeros((nwb, vr, tk), BF16)
    ones_rows = jnp.where(lax.broadcasted_iota(jnp.int32, (vr - dh, tk), 0) == 0, 1.0, 0.0).astype(BF16)

    def prep(blk, carry):
        r0 = pl.multiple_of(blk * tk, tk)
        cos_t = cos_ref[0, pl.ds(r0, tk), :]
        sin_t = sin_ref[0, pl.ds(r0, tk), :]
        cq = jnp.concatenate([cos_t] * (qw // LANE), axis=1)
        sq = jnp.concatenate([sin_t] * (qw // LANE), axis=1)
        qr = _rope_apply(q_ref[pl.ds(r0, tk), :], cq, sq, perm_ref[...]) * scale
        qt_ref[blk] = qr.T.astype(BF16)
        ck = jnp.where(lane < dh, cos_t, 1.0)
        sk = jnp.where(lane < dh, sin_t, 0.0)
        kv = kv_ref[pl.ds(r0, tk), :]
        perm_kv = perm_ref[0:LANE, 0:LANE]
        csrc_ref[pl.ds(r0, tk), :] = _rope_apply(kv[:, 0:LANE], ck, sk, perm_kv)
        kv_s = _rope_apply(kv[:, LANE:2 * LANE], ck, sk, perm_kv)
        kslc_ref[pl.ds(r0, tk), :] = kv_s[:, :dh].astype(BF16)
        vslct_ref[blk] = jnp.concatenate([kv_s.T[dh:, :].astype(BF16), ones_rows], axis=0)
        kv_w = _rope_apply(kv[:, 2 * LANE:3 * LANE], ck, sk, perm_kv)
        kwin_ref[pl.ds(pl.multiple_of(WINDOW + r0, tk), tk), :] = kv_w[:, :dh].astype(BF16)
        vwint_ref[nwb + blk] = jnp.concatenate([kv_w.T[dh:, :].astype(BF16), ones_rows], axis=0)
        return carry

    lax.fori_loop(0, seq // tk, prep, 0)

    for l in range(L_CMP):
        ccat_ref[:, l * LANE:(l + 1) * LANE] = (
            csrc_ref[pl.ds(l, nc, stride=STRIDE_CMP), :] + pe_ref[l:l + 1, :])
    kvc = _dot(ccat_ref[...], wc_ref[...])
    kc_ref[...] = kvc[:, :dh].astype(BF16)
    vct_ref[...] = kvc.T[dh:, :].astype(BF16)

    n_sub = lax.broadcasted_iota(jnp.int32, (nc, 1), 0)
    j_sub = lax.broadcasted_iota(jnp.int32, (ns, 1), 0)
    k_sub = lax.broadcasted_iota(jnp.int32, (kb, 1), 0)
    c_sub = lax.broadcasted_iota(jnp.int32, (band, 1), 0)
    q_lane = lax.broadcasted_iota(jnp.int32, (1, tq), 1)
    win_static = (c_sub - WINDOW <= q_lane) & (c_sub > q_lane)
    heads = [slice(h * tq, (h + 1) * tq) for h in range(hpg)]

    def q_tile(i, carry):
        r0 = pl.multiple_of(i * tq, tq)
        b0 = i * (tq // tk)
        qblk = [qt_ref[b0 + jb] for jb in range(tq // tk)]
        qt4 = jnp.concatenate([qb[h * dh:(h + 1) * dh, :] for h in range(hpg) for qb in qblk],
                              axis=1)
        t_lane = r0 + q_lane

        s = _dot(kc_ref[...], qt4)
        cmask = (n_sub * STRIDE_CMP + (L_CMP - 1)) <= t_lane
        ps = [_masked_softmax0(s[:, sl], cmask) for sl in heads]
        o_cmp = _dot(vct_ref[...], jnp.concatenate([p.astype(BF16) for p in ps], axis=1))
        psum = ps[0]
        for p in ps[1:]:
            psum = psum + p

        imp = _dot(ovl_ref[...], psum, hi=True)
        cur = t_lane // L_SLC
        causal_blk = j_sub <= cur
        forced = (j_sub == 0) | (j_sub == cur) | (j_sub == cur - 1)
        impv = jnp.where(forced, 1e9, jnp.where(causal_blk, imp, -1e9))
        rank = jnp.zeros((ns, tq), F32)
        for b in range(ns):
            row = impv[b:b + 1, :]
            beats = (row > impv) | ((row == impv) & (b < j_sub))
            rank = rank + jnp.where(beats, 1.0, 0.0)
        sel = jnp.where((rank < n_top) & causal_blk, 1.0, 0.0)
        sel_ref[...] = sel

        s = _dot(kwin_ref[pl.ds(r0, band), :], qt4)
        wmask = win_static & (c_sub >= WINDOW - r0)
        wmb = jnp.where(wmask, 1.0, 0.0).astype(BF16)
        ps = []
        for sl in heads:
            sh = jnp.where(wmask, s[:, sl], -1e30)
            ps.append(_exp_masked_bf16(sh, jnp.max(sh, axis=0, keepdims=True), wmb))
        p4 = jnp.concatenate(ps, axis=1)
        ow = _dot(vwint_ref[b0], p4[0:tk, :])
        for jb in range(1, band // tk):
            ow = ow + _dot(vwint_ref[b0 + jb], p4[jb * tk:(jb + 1) * tk, :])
        o_win = ow[0:dh] * (1.0 / jnp.maximum(ow[dh:dh + 1], 1e-30))

        gt = _dot(gsel_ref[0], jax.nn.sigmoid(sm_ref[pl.ds(r0, tq), :]), NT, hi=True)
        gate = lambda br: jnp.concatenate(
            [jnp.broadcast_to(gt[br * hpg + h:br * hpg + h + 1], (dh, tq)) for h in range(hpg)], axis=1)
        o_cw = gate(0) * o_cmp + gate(2) * o_win
        g_slc = gate(1)

        def kv_step(j, st):
            m, acc = st
            k0 = pl.multiple_of(j * kb, kb)
            s = _dot(kslc_ref[pl.ds(k0, kb), :], qt4)
            sel_rows = sel_ref[pl.ds(pl.multiple_of(j * nsb, nsb), nsb), :]
            selm = jnp.concatenate([jnp.broadcast_to(sel_rows[b:b + 1, :], (L_SLC, tq)) for b in range(nsb)], axis=0)
            msk = (selm > 0.5) & (k0 + k_sub <= t_lane)
            mb = jnp.where(msk, 1.0, 0.0).astype(BF16)
            ms, als, ps = [], [], []
            for sl in heads:
                sh = jnp.where(msk, s[:, sl], -1e30)
                mn = jnp.maximum(m[:, sl], jnp.max(sh, axis=0, keepdims=True))
                ms.append(mn)
                als.append(jnp.exp2(m[:, sl] - mn))
                ps.append(_exp_masked_bf16(sh, mn, mb))
            p4 = jnp.concatenate(ps, axis=1)
            pv = _dot(vslct_ref[j * (kb // tk)], p4[0:tk, :])
            for hf in range(1, kb // tk):
                pv = pv + _dot(vslct_ref[j * (kb // tk) + hf], p4[hf * tk:(hf + 1) * tk, :])
            return jnp.concatenate(ms, axis=1), jnp.concatenate(als, axis=1) * acc + pv

        init = (jnp.full((1, hpg * tq), -1e30, F32), jnp.zeros((vr, hpg * tq), F32))
        _, acc_s = lax.fori_loop(0, (r0 + tq + kb - 1) // kb, kv_step, init)
        o_slc = acc_s[0:dh] * (1.0 / jnp.maximum(acc_s[dh:dh + 1], 1e-30))

        out_t = o_cw + g_slc * o_slc
        out_t = jnp.concatenate([out_t[:, sl] for sl in heads], axis=0)
        o_ref[pl.ds(r0, tq), :] = out_t.T.astype(o_ref.dtype)
        return carry

    lax.fori_loop(0, seq // tq, q_tile, 0)


def _nsa_constants(seq):
    nc = seq // STRIDE_CMP
    ns = seq // L_SLC
    cs = np.arange(nc) * STRIDE_CMP
    js = np.arange(ns) * L_SLC
    ovl_t = ((cs[None, :] < js[:, None] + L_SLC) & (cs[None, :] + L_CMP > js[:, None])).astype(np.float32)
    gsel = np.zeros((NSA_G, 16, LANE), np.float32)
    for g in range(NSA_G):
        for h in range(NSA_HPG):
            for j in range(3):
                gsel[g, j * NSA_HPG + h, SM_GATE + (g * NSA_HPG + h) * 3 + j] = 1.0
    qw = NSA_HPG * NSA_DH
    half = ROPE_DIM // 2
    perm = np.zeros((qw, qw), np.float32)
    for l in range(qw):
        d = l % NSA_DH
        if d < ROPE_DIM:
            perm[l + half if d < half else l - half, l] = 1.0
    return jnp.asarray(ovl_t), jnp.asarray(gsel), jnp.asarray(perm, dtype=BF16)


def _nsa(proj, q_blk, kv_blk, sm_blk, bsz, seq, cos_t, sin_t, pe_k, pe_v, wk_cmp, wv_cmp, consts):
    t = bsz * seq
    dh = NSA_DH
    qw = NSA_HPG * dh
    tq, tk = NSA_TQ, NSA_TK
    nc = seq // STRIDE_CMP
    ns = seq // L_SLC
    ovl_t, gsel, perm = consts
    pe = jnp.concatenate([pe_k, pe_v], axis=1)
    zero = jnp.zeros((L_CMP, dh, dh), F32)
    wc = jnp.concatenate([jnp.concatenate([wk_cmp, zero], axis=2),
                          jnp.concatenate([zero, wv_cmp], axis=2)], axis=1)
    wc = wc.reshape(L_CMP * LANE, LANE).astype(BF16)
    kern = functools.partial(_nsa_kernel, seq=seq)
    full = lambda shape: pl.BlockSpec(shape, lambda b, g: tuple(0 for _ in shape))
    return pl.pallas_call(
        kern,
        grid=(bsz, NSA_G),
        in_specs=[pl.BlockSpec((seq, qw), lambda b, g: (b, q_blk + g)),
                  pl.BlockSpec((seq, 3 * LANE), lambda b, g: (b, kv_blk + g)),
                  pl.BlockSpec((seq, LANE), lambda b, g: (b, sm_blk)),
                  pl.BlockSpec((1, seq, LANE), lambda b, g: (b, 0, 0)),
                  pl.BlockSpec((1, seq, LANE), lambda b, g: (b, 0, 0)),
                  full((L_CMP, LANE)),
                  full((L_CMP * LANE, LANE)),
                  full((ns, nc)),
                  pl.BlockSpec((1, 16, LANE), lambda b, g: (g, 0, 0)),
                  full((qw, qw))],
        out_specs=pl.BlockSpec((seq, qw), lambda b, g: (b, g)),
        out_shape=jax.ShapeDtypeStruct((t, NSA_G * qw), BF16),
        scratch_shapes=[pltpu.VMEM((seq // tk, qw, tk), BF16),
                        pltpu.VMEM((seq + L_CMP, LANE), F32),
                        pltpu.VMEM((nc, L_CMP * LANE), F32),
                        pltpu.VMEM((nc, dh), BF16),
                        pltpu.VMEM((dh, nc), BF16),
                        pltpu.VMEM((seq, dh), BF16),
                        pltpu.VMEM((seq // tk, NSA_VR, tk), BF16),
                        pltpu.VMEM((WINDOW + seq, dh), BF16),
                        pltpu.VMEM(((WINDOW + seq) // tk, NSA_VR, tk), BF16),
                        pltpu.VMEM((ns, tq), F32)],
        compiler_params=_params("parallel", "parallel"),
        name="nsa",
    )(proj, proj, proj, cos_t, sin_t, pe, wc, ovl_t, gsel, perm)


def _dot3(a, b):
    a_hi = a.astype(BF16)
    a_lo = (a - a_hi.astype(F32)).astype(BF16)
    b_hi = b.astype(BF16)
    b_lo = (b - b_hi.astype(F32)).astype(BF16)
    d = lambda x, y: jnp.dot(x, y, preferred_element_type=F32)
    return d(a_hi, b_hi) + (d(a_lo, b_hi) + d(a_hi, b_lo))


def _unit_lower_solve(lmats, rhss):
    c, width = rhss[0].shape
    bs = GDN_SOLVE_BLOCK
    done = [[] for _ in rhss]
    for b in range(c // bs):
        r0 = b * bs
        xbs = []
        for lmat, rhs, dn in zip(lmats, rhss, done):
            xb = rhs[r0:r0 + bs]
            if b:
                partial = jnp.concatenate(dn + [jnp.zeros((c - r0, width), F32)], axis=0)
                xb = xb - _dot3(lmat[r0:r0 + bs, :], partial)
            xbs.append(xb)
        for lmat, xb, dn in zip(lmats, xbs, done):
            lb = lmat[r0:r0 + bs, r0:r0 + bs]
            groups = [xb[8 * g:8 * g + 8] for g in range(bs // 8)]
            for j in range(bs - 1):
                row = groups[j // 8][j % 8:j % 8 + 1, :]
                for g in range(j // 8, bs // 8):
                    if 8 * g + 7 > j:
                        groups[g] = groups[g] - lb[8 * g:8 * g + 8, j:j + 1] * row
            dn.append(jnp.concatenate(groups, axis=0))
    return [jnp.concatenate(dn, axis=0) for dn in done]


def _gdn_kernel(qkv_ref, z_ref, sm_ref, cw_ref, alog_ref, dtb_ref, ng_ref, o_ref, buf_ref, xc_ref, st_ref,
                *, k_len):
    c = GDN_CHUNK
    rows = c * GDN_STEP_CHUNKS
    dk, dv = GDN_DK, GDN_DV

    @pl.when(pl.program_id(1) == 0)
    def _():
        buf_ref[0:GDN_HALO, :] = jnp.zeros((GDN_HALO, GDN_QKV), F32)
        st_ref[...] = jnp.zeros(st_ref.shape, F32)

    x = qkv_ref[...]
    buf_ref[GDN_HALO:GDN_HALO + rows, :] = x
    base = GDN_HALO - (k_len - 1)
    acc = jnp.zeros((rows, GDN_QKV), F32)
    for k in range(k_len):
        acc = acc + cw_ref[k:k + 1, :] * buf_ref[base + k:base + k + rows, :]
    buf_ref[0:GDN_HALO, :] = x[rows - GDN_HALO:rows, :]
    xc_ref[...] = acc * jax.nn.sigmoid(acc)

    sm = sm_ref[...]
    beta_all = jax.nn.sigmoid(sm)
    g_all = -jnp.exp(alog_ref[...]) * jax.nn.softplus(sm + dtb_ref[...])
    ri = lax.broadcasted_iota(jnp.int32, (rows, rows), 0)
    rj = lax.broadcasted_iota(jnp.int32, (rows, rows), 1)
    same_chunk_incl = (ri >= rj) & (ri // c == rj // c)
    gc_all = _dot(jnp.where(same_chunk_incl, 1.0, 0.0), g_all, hi=True)
    gc_rows = gc_all.T
    ii = lax.broadcasted_iota(jnp.int32, (c, c), 0)
    jj = lax.broadcasted_iota(jnp.int32, (c, c), 1)
    incl = ii >= jj
    strict = ii > jj

    pre, lmats, rhss = [], [], []
    for n in range(GDN_STEP_CHUNKS):
        rs = slice(n * c, (n + 1) * c)
        for h in range(GDN_H):
            gcol = gc_all[rs, SM_A + h:SM_A + h + 1]
            bcol = beta_all[rs, SM_BETA + h:SM_BETA + h + 1]
            grow = gc_rows[SM_A + h:SM_A + h + 1, rs]
            decay = jnp.exp(jnp.where(incl, gcol - grow, -jnp.inf))
            q = xc_ref[rs, h * dk:(h + 1) * dk]
            k = xc_ref[rs, GDN_H * dk + h * dk:GDN_H * dk + (h + 1) * dk]
            v = xc_ref[rs, 2 * GDN_H * dk + h * dv:2 * GDN_H * dk + (h + 1) * dv]
            q = q * lax.rsqrt(jnp.sum(q * q, axis=-1, keepdims=True) + EPS) * (dk ** -0.5)
            k = k * lax.rsqrt(jnp.sum(k * k, axis=-1, keepdims=True) + EPS)
            eg = jnp.exp(gcol)
            lmat = jnp.where(strict, bcol * _dot(k, k, NT) * decay, 0.0)
            qk = jnp.where(incl, _dot(q, k, NT) * decay, 0.0)
            glast = gcol[c - 1:c, :]
            lmats.append(lmat)
            rhss.append(jnp.concatenate([k * (bcol * eg), v * bcol], axis=1))
            pre.append((q * eg, k * jnp.exp(glast - gcol), qk, jnp.exp(glast)))
    sols = _unit_lower_solve(lmats, rhss)

    z = z_ref[...]
    for n in range(GDN_STEP_CHUNKS):
        rs = slice(n * c, (n + 1) * c)
        idx = range(n * GDN_H, (n + 1) * GDN_H)
        states = [st_ref[h] for h in range(GDN_H)]
        v_news = [sols[i][:, dk:] - _dot(sols[i][:, :dk], st) for i, st in zip(idx, states)]
        o_state = [_dot(pre[i][0], st) for i, st in zip(idx, states)]
        o_intra = [_dot(pre[i][2], vn) for i, vn in zip(idx, v_news)]
        s_add = [_dot(pre[i][1], vn, TN) for i, vn in zip(idx, v_news)]
        for h, i in enumerate(idx):
            st_ref[h] = pre[i][3] * states[h] + s_add[h]
            o = o_state[h] + o_intra[h]
            o = o * lax.rsqrt(jnp.mean(o * o, axis=-1, keepdims=True) + EPS) * ng_ref[...]
            zh = z[rs, h * dv:(h + 1) * dv]
            o_ref[rs, h * dv:(h + 1) * dv] = (o * (zh * jax.nn.sigmoid(zh))).astype(o_ref.dtype)


def _gdn(proj, qkv_blk, z_blk, sm_blk, bsz, seq, conv_w, a_log, dt_bias, norm_g):
    t = bsz * seq
    c = GDN_CHUNK * GDN_STEP_CHUNKS
    nch = seq // c
    k_len = conv_w.shape[0]
    alog = jnp.zeros((1, LANE), F32).at[0, SM_A:SM_A + GDN_H].set(a_log)
    dtb = jnp.zeros((1, LANE), F32).at[0, SM_A:SM_A + GDN_H].set(dt_bias)
    kern = functools.partial(_gdn_kernel, k_len=k_len)
    full = lambda shape: pl.BlockSpec(shape, lambda b, n: tuple(0 for _ in shape))
    return pl.pallas_call(
        kern,
        grid=(bsz, nch),
        in_specs=[pl.BlockSpec((c, GDN_QKV), lambda b, n: (b * nch + n, qkv_blk)),
                  pl.BlockSpec((c, GDN_WIDTH), lambda b, n: (b * nch + n, z_blk)),
                  pl.BlockSpec((c, LANE), lambda b, n: (b * nch + n, sm_blk)),
                  full((k_len, GDN_QKV)), full((1, LANE)), full((1, LANE)), full((1, GDN_DV))],
        out_specs=pl.BlockSpec((c, GDN_WIDTH), lambda b, n: (b * nch + n, 0)),
        out_shape=jax.ShapeDtypeStruct((t, GDN_WIDTH), BF16),
        scratch_shapes=[pltpu.VMEM((GDN_HALO + c, GDN_QKV), F32),
                        pltpu.VMEM((c, GDN_QKV), F32),
                        pltpu.VMEM((GDN_H, GDN_DK, GDN_DV), F32)],
        compiler_params=_params("parallel", "arbitrary"),
        name="gdn",
    )(proj, proj, proj, conv_w, alog, dtb, norm_g.reshape(1, -1))


def _merge_kernel(x_ref, mod_ref, yc_ref, yn_ref, yg_ref, gm_ref, wc_ref, wn_ref, wg_ref, wo_ref, o_ref):
    d = x_ref.shape[1]
    gm = jax.nn.sigmoid(gm_ref[...])
    merged = (gm[:, 0:d] * jnp.dot(yc_ref[...], wc_ref[...], preferred_element_type=F32)
              + gm[:, d:2 * d] * jnp.dot(yn_ref[...], wn_ref[...], preferred_element_type=F32)
              + gm[:, 2 * d:3 * d] * jnp.dot(yg_ref[...], wg_ref[...], preferred_element_type=F32))
    gt = mod_ref[0][2:3]
    o_ref[...] = x_ref[...] + gt * _dot(merged, wo_ref[...])


def _merge(x2, mod_l, y_conv, y_nsa, y_gdn, proj, gm_blk, w_c, w_n, w_g, w_o, seq, tm=512):
    t, d = x2.shape
    per_b = seq // tm
    row = lambda w: pl.BlockSpec((tm, w), lambda i: (i, 0))
    wspec = lambda w: pl.BlockSpec(w.shape, lambda i: (0, 0))
    return pl.pallas_call(
        _merge_kernel,
        grid=(t // tm,),
        in_specs=[row(d),
                  pl.BlockSpec((1, N_MOD, d), lambda i: (i // per_b, 0, 0)),
                  row(y_conv.shape[1]), row(y_nsa.shape[1]), row(y_gdn.shape[1]),
                  pl.BlockSpec((tm, 3 * d), lambda i: (i, gm_blk)),
                  wspec(w_c), wspec(w_n), wspec(w_g), wspec(w_o)],
        out_specs=row(d),
        out_shape=jax.ShapeDtypeStruct((t, d), F32),
        compiler_params=_params("parallel"),
        name="merge",
    )(x2, mod_l, y_conv, y_nsa, y_gdn, proj, w_c, w_n, w_g, w_o)


def _ffn_kernel(x_ref, mod_ref, g_ref, wg_ref, wu_ref, wo_ref, fg_ref, o_ref, h_ref, acc_ref, *, final):
    j = pl.program_id(1)

    @pl.when(j == 0)
    def _():
        m = mod_ref[0]
        h_ref[...] = _norm_mod(x_ref[...], g_ref[...], m[3:4], m[4:5]).astype(BF16)
        acc_ref[...] = jnp.zeros(acc_ref.shape, F32)

    h = h_ref[...]
    gate = jnp.dot(h, wg_ref[...], preferred_element_type=F32)
    up = jnp.dot(h, wu_ref[...], preferred_element_type=F32)
    act = (gate * jax.nn.sigmoid(gate)) * up
    acc_ref[...] += _dot(act, wo_ref[...])

    @pl.when(j == pl.num_programs(1) - 1)
    def _():
        y = x_ref[...] + mod_ref[0][5:6] * acc_ref[...]
        if final:
            y = y * lax.rsqrt(jnp.mean(y * y, axis=-1, keepdims=True) + EPS) * fg_ref[...]
        o_ref[...] = y


def _ffn(x2, mod_l, g, w_in, w_out, final_g, seq, final, tm=512, th=1408):
    t, d = x2.shape
    hid = w_out.shape[0]
    per_b = seq // tm
    nh = hid // th
    kern = functools.partial(_ffn_kernel, final=final)
    return pl.pallas_call(
        kern,
        grid=(t // tm, nh),
        in_specs=[pl.BlockSpec((tm, d), lambda i, j: (i, 0)),
                  pl.BlockSpec((1, N_MOD, d), lambda i, j: (i // per_b, 0, 0)),
                  pl.BlockSpec((1, d), lambda i, j: (0, 0)),
                  pl.BlockSpec((d, th), lambda i, j: (0, j)),
                  pl.BlockSpec((d, th), lambda i, j: (0, nh + j)),
                  pl.BlockSpec((th, d), lambda i, j: (j, 0)),
                  pl.BlockSpec((1, d), lambda i, j: (0, 0))],
        out_specs=pl.BlockSpec((tm, d), lambda i, j: (i, 0)),
        out_shape=jax.ShapeDtypeStruct((t, d), F32),
        scratch_shapes=[pltpu.VMEM((tm, d), BF16), pltpu.VMEM((tm, d), F32)],
        compiler_params=_params("parallel", "arbitrary"),
        name="ffn",
    )(x2, mod_l, g, w_in, w_in, w_out, final_g)


def _in_layout(d):
    segs = {}
    off = 0
    for name, width in (("gm", 3 * d), ("conv", 2 * C_CONV), ("q", NSA_WIDTH), ("qkv", GDN_QKV),
                        ("z", GDN_WIDTH), ("small", LANE), ("pad", LANE),
                        ("kv", 6 * NSA_G * NSA_DH)):
        segs[name] = off
        off += width
    return segs, off


def _rearrange_in_proj(w_in, b_in, d):
    splits = (2 * C_CONV, NSA_WIDTH, 6 * NSA_G * NSA_DH, 3 * NSA_HEADS, GDN_QKV, GDN_WIDTH, GDN_H, GDN_H, 3 * d)
    offs = np.concatenate([[0], np.cumsum(splits)])
    names = ("conv", "q", "kv", "gnsa", "qkv", "z", "beta", "a", "gm")
    segs, total = _in_layout(d)

    def arrange(a):
        src = {n: a[..., int(offs[i]):int(offs[i + 1])] for i, n in enumerate(names)}
        lead = a.shape[:-1]
        kv = jnp.swapaxes(src["kv"].reshape(lead + (6, NSA_G, NSA_DH)), -3, -2).reshape(lead + (-1,))
        small_pad = jnp.zeros(lead + (LANE - 3 * NSA_HEADS - 2 * GDN_H,), a.dtype)
        pad = jnp.zeros(lead + (LANE,), a.dtype)
        out = jnp.concatenate([src["gm"], src["conv"], src["q"], src["qkv"], src["z"],
                               src["gnsa"], src["beta"], src["a"], small_pad, pad, kv], axis=-1)
        assert out.shape[-1] == total
        return out

    return arrange(w_in).astype(BF16), arrange(b_in)[:, None, :], segs


def kernel(x, c, positions, ada_w, ada_b, norm_mix_g, norm_ffn_g, w_in, b_in, conv_dw_w, conv_dw_b, conv_ln_g, conv_ln_b, nsa_pe_k, nsa_pe_v, nsa_wk_cmp, nsa_wv_cmp, gdn_conv_w, gdn_a_log, gdn_dt_bias, gdn_norm_g, w_up_conv, w_up_nsa, w_up_gdn, w_o, ffn_w_in, ffn_w_out, final_norm_g):
    bsz, seq, d = x.shape
    depth = ada_w.shape[0]
    t = bsz * seq

    mod = _modulation(c, ada_w, ada_b).reshape(depth, bsz, N_MOD, d)
    cos_t, sin_t = _rope_tables(positions)
    nsa_consts = _nsa_constants(seq)
    w_in_r, b_in_r, segs = _rearrange_in_proj(w_in, b_in, d)
    blk = lambda name, width: segs[name] // width
    qw = NSA_HPG * NSA_DH

    x2 = x.reshape(t, d)
    for l in range(depth):
        proj = _in_projection(x2, mod[l], norm_mix_g[l].reshape(1, d), w_in_r[l], b_in_r[l], seq)
        y_conv = _conformer_conv(proj, blk("conv", 2 * C_CONV), bsz, seq,
                                 conv_dw_w[l], conv_dw_b[l], conv_ln_g[l], conv_ln_b[l])
        y_nsa = _nsa(proj, blk("q", qw), blk("kv", 3 * LANE), blk("small", LANE), bsz, seq, cos_t, sin_t,
                     nsa_pe_k[l], nsa_pe_v[l], nsa_wk_cmp[l], nsa_wv_cmp[l], nsa_consts)
        y_gdn = _gdn(proj, blk("qkv", GDN_QKV), blk("z", GDN_WIDTH), blk("small", LANE), bsz, seq,
                     gdn_conv_w[l], gdn_a_log[l], gdn_dt_bias[l], gdn_norm_g[l])
        x2 = _merge(x2, mod[l], y_conv, y_nsa, y_gdn, proj, blk("gm", 3 * d),
                    w_up_conv[l].astype(BF16), w_up_nsa[l].astype(BF16), w_up_gdn[l].astype(BF16),
                    w_o[l].astype(BF16), seq)
        x2 = _ffn(x2, mod[l], norm_ffn_g[l].reshape(1, d), ffn_w_in[l].astype(BF16),
                  ffn_w_out[l].astype(BF16), final_norm_g.reshape(1, d), seq, final=(l == depth - 1))
    return x2.reshape(bsz, seq, d)
```

```python
import functools

import numpy as np
import jax
import jax.numpy as jnp
from jax import lax
from jax.experimental import pallas as pl
from jax.experimental.pallas import tpu as pltpu

F32 = jnp.float32
BF16 = jnp.bfloat16
HI = lax.Precision.HIGHEST
NT = (((1,), (1,)), ((), ()))
TN = (((0,), (0,)), ((), ()))

EPS = 1e-6
LOG2_E = 1.4426950408889634
N_MOD = 6
C_CONV = 512
CONV_HALO = 32
NSA_HEADS = 8
NSA_G = 2
NSA_HPG = NSA_HEADS // NSA_G
NSA_DH = 64
NSA_WIDTH = NSA_HEADS * NSA_DH
ROPE_DIM = NSA_DH // 4
ROPE_THETA = 500000.0
L_CMP = 32
STRIDE_CMP = 16
L_SLC = 64
N_SLC_TOP = 8
WINDOW = 512
NSA_TQ = 512
NSA_TK = 128
NSA_PREP_UNROLL = 4
NSA_KB = 512
NSA_VR = NSA_DH + 16
GDN_H = 4
GDN_DK = 128
GDN_DV = 128
GDN_WIDTH = GDN_H * GDN_DV
GDN_QKV = GDN_H * (2 * GDN_DK + GDN_DV)
GDN_CHUNK = 64
GDN_HALO = 8
GDN_STEP_CHUNKS = 4
GDN_SOLVE_BLOCK = 16
SM_GATE = 0
SM_BETA = 24
SM_A = 28
LANE = 128

VMEM_LIMIT = 56 * 1024 * 1024


def _dot(a, b, dims=None, hi=False):
    if dims is None:
        dims = (((a.ndim - 1,), (0,)), ((), ()))
    if hi:
        return lax.dot_general(a.astype(F32), b.astype(F32), dims, precision=HI,
                               preferred_element_type=F32)
    return lax.dot_general(a.astype(BF16), b.astype(BF16), dims, preferred_element_type=F32)


def _masked_softmax0(s, mask):
    s = jnp.where(mask, s, -1e30)
    m = jnp.max(s, axis=0, keepdims=True)
    p = jnp.where(mask, jnp.exp2(s - m), 0.0)
    return p * (1.0 / jnp.maximum(jnp.sum(p, axis=0, keepdims=True), 1e-30))


def _exp_masked_bf16(sh, m, mb):
    return jnp.exp2((sh - m).astype(BF16)) * mb


def _params(*sem):
    return pltpu.CompilerParams(dimension_semantics=sem, vmem_limit_bytes=VMEM_LIMIT)


def _mod_kernel(c_ref, w_ref, b_ref, o_ref):
    c = c_ref[...]
    ca = c * jax.nn.sigmoid(c)
    o_ref[0] = _dot(ca, w_ref[0], hi=True) + b_ref[0]


def _modulation(c, ada_w, ada_b):
    depth, d, n = ada_w.shape
    bsz = c.shape[0]
    tn = 1536
    return pl.pallas_call(
        _mod_kernel,
        grid=(depth, n // tn),
        in_specs=[pl.BlockSpec((bsz, d), lambda l, j: (0, 0)),
                  pl.BlockSpec((1, d, tn), lambda l, j: (l, 0, j)),
                  pl.BlockSpec((1, 1, tn), lambda l, j: (l, 0, j))],
        out_specs=pl.BlockSpec((1, bsz, tn), lambda l, j: (l, 0, j)),
        out_shape=jax.ShapeDtypeStruct((depth, bsz, n), F32),
        compiler_params=_params("parallel", "parallel"),
        name="adaln_mod",
    )(c, ada_w, ada_b.reshape(depth, 1, n))


def _norm_mod(x, g, sh, sc):
    y = x * lax.rsqrt(jnp.mean(x * x, axis=-1, keepdims=True) + EPS)
    return (y * g) * (1.0 + sc) + sh


def _inproj_kernel(x_ref, mod_ref, g_ref, w_ref, b_ref, o_ref, h_ref):
    @pl.when(pl.program_id(1) == 0)
    def _():
        m = mod_ref[0]
        h_ref[...] = _norm_mod(x_ref[...], g_ref[...], m[0:1], m[1:2]).astype(BF16)

    o_ref[...] = jnp.dot(h_ref[...], w_ref[...], preferred_element_type=F32) + b_ref[...]


def _in_projection(x2, mod_l, g, w, b, seq, tm=1024, tn=1536):
    t, d = x2.shape
    n = w.shape[1]
    per_b = seq // tm
    return pl.pallas_call(
        _inproj_kernel,
        grid=(t // tm, n // tn),
        in_specs=[pl.BlockSpec((tm, d), lambda i, j: (i, 0)),
                  pl.BlockSpec((1, N_MOD, d), lambda i, j: (i // per_b, 0, 0)),
                  pl.BlockSpec((1, d), lambda i, j: (0, 0)),
                  pl.BlockSpec((d, tn), lambda i, j: (0, j)),
                  pl.BlockSpec((1, tn), lambda i, j: (0, j))],
        out_specs=pl.BlockSpec((tm, tn), lambda i, j: (i, j)),
        out_shape=jax.ShapeDtypeStruct((t, n), F32),
        scratch_shapes=[pltpu.VMEM((tm, d), BF16)],
        compiler_params=_params("parallel", "arbitrary"),
        name="in_proj",
    )(x2, mod_l, g, w, b)


def _conv_kernel(u_ref, w_ref, b_ref, lg_ref, lb_ref, o_ref, buf_ref, *, ts, k_len, rc):
    c = C_CONV

    @pl.when(pl.program_id(1) == 0)
    def _():
        buf_ref[0:CONV_HALO, :] = jnp.zeros((CONV_HALO, c), F32)

    @pl.when(pl.program_id(1) > 0)
    def _():
        buf_ref[0:CONV_HALO, :] = buf_ref[ts:ts + CONV_HALO, :]

    u = u_ref[...]
    buf_ref[CONV_HALO:CONV_HALO + ts, :] = u[:, :c] * jax.nn.sigmoid(u[:, c:])
    base = CONV_HALO - (k_len - 1)

    def chunk(ci, carry):
        r0 = pl.multiple_of(ci * rc, rc)
        win = buf_ref[pl.ds(r0, rc + CONV_HALO), :]
        acc = jnp.zeros((rc // 8, 8, c), F32)
        for ph in range(8):
            taps = [k for k in range(k_len) if (base + k) % 8 == ph]
            if not taps:
                continue
            span = max(base + k for k in taps) - ph + rc
            shifted = win[ph:ph + span, :]
            for k in taps:
                off = base + k - ph
                acc = acc + w_ref[k][None] * shifted[off:off + rc, :].reshape(rc // 8, 8, c)
        y = acc.reshape(rc, c) + b_ref[...]
        mu = jnp.mean(y, axis=-1, keepdims=True)
        var = jnp.mean(jnp.square(y - mu), axis=-1, keepdims=True)
        y = (y - mu) * lax.rsqrt(var + EPS) * lg_ref[...] + lb_ref[...]
        o_ref[pl.ds(r0, rc), :] = (y * jax.nn.sigmoid(y)).astype(o_ref.dtype)
        return carry

    lax.fori_loop(0, ts // rc, chunk, 0)


def _conformer_conv(proj, col_blk, bsz, seq, dw_w, dw_b, ln_g, ln_b, ts=512, rc=128):
    t = bsz * seq
    k_len = dw_w.shape[0]
    per_b = seq // ts
    kern = functools.partial(_conv_kernel, ts=ts, k_len=k_len, rc=rc)
    vec = lambda: pl.BlockSpec((1, C_CONV), lambda b, s: (0, 0))
    return pl.pallas_call(
        kern,
        grid=(bsz, per_b),
        in_specs=[pl.BlockSpec((ts, 2 * C_CONV), lambda b, s: (b * per_b + s, col_blk)),
                  pl.BlockSpec((k_len, 8, C_CONV), lambda b, s: (0, 0, 0)),
                  vec(), vec(), vec()],
        out_specs=pl.BlockSpec((ts, C_CONV), lambda b, s: (b * per_b + s, 0)),
        out_shape=jax.ShapeDtypeStruct((t, C_CONV), BF16),
        scratch_shapes=[pltpu.VMEM((CONV_HALO + ts, C_CONV), F32)],
        compiler_params=_params("parallel", "arbitrary"),
        name="conformer_conv",
    )(proj, jnp.broadcast_to(dw_w[:, None, :], (k_len, 8, C_CONV)),
      dw_b.reshape(1, -1), ln_g.reshape(1, -1), ln_b.reshape(1, -1))


def _rope_table_kernel(pos_ref, inv_ref, sign_ref, cos_ref, sin_ref):
    ang = pos_ref[0].astype(F32) * inv_ref[...]
    cos_ref[0] = jnp.cos(ang)
    sin_ref[0] = jnp.sin(ang) * sign_ref[...]


def _rope_tables(positions):
    bsz, seq = positions.shape
    half = ROPE_DIM // 2
    inv = ROPE_THETA ** (-jnp.arange(half, dtype=F32) * 2.0 / ROPE_DIM)
    d = np.arange(LANE) % NSA_DH
    inv_lane = jnp.where(jnp.asarray(d < ROPE_DIM), inv[d % half], 0.0).reshape(1, LANE)
    sign_lane = jnp.asarray(np.where(d < half, -1.0, np.where(d < ROPE_DIM, 1.0, 0.0)),
                            dtype=F32).reshape(1, LANE)
    out = jax.ShapeDtypeStruct((bsz, seq, LANE), F32)
    return pl.pallas_call(
        _rope_table_kernel,
        grid=(bsz,),
        in_specs=[pl.BlockSpec((1, seq, 1), lambda b: (b, 0, 0)),
                  pl.BlockSpec((1, LANE), lambda b: (0, 0)),
                  pl.BlockSpec((1, LANE), lambda b: (0, 0))],
        out_specs=[pl.BlockSpec((1, seq, LANE), lambda b: (b, 0, 0))] * 2,
        out_shape=[out, out],
        compiler_params=_params("parallel"),
        name="rope_tables",
    )(positions.reshape(bsz, seq, 1), inv_lane, sign_lane)


def _rope_apply(x, cos_t, sin_t, perm):
    partner = jnp.dot(x.astype(BF16), perm, preferred_element_type=F32)
    return x * cos_t + partner * sin_t


def _nsa_kernel(q_ref, kv_ref, sm_ref, cos_ref, sin_ref, pe_ref, wc_ref, ovl_ref, gsel_ref, perm_ref,
                o_ref,
                qt_ref, csrc_ref, ccat_ref, kc_ref, vct_ref, kslc_ref, vslct_ref, kwin_ref, vwint_ref,
                sel_ref, *, seq):
    tq, tk, kb, dh, hpg, vr = NSA_TQ, NSA_TK, NSA_KB, NSA_DH, NSA_HPG, NSA_VR
    nc = seq // STRIDE_CMP
    ns = seq // L_SLC
    nwb = WINDOW // tk
    band = WINDOW + tq
    scale = dh ** -0.5 * LOG2_E
    qw = hpg * dh
    n_top = float(min(N_SLC_TOP, ns))
    nsb = kb // L_SLC
    lane = lax.broadcasted_iota(jnp.int32, (1, LANE), 1)

    csrc_ref[seq:seq + L_CMP, :] = jnp.zeros((L_CMP, LANE), F32)
    kwin_ref[0:WINDOW, :] = jnp.zeros((WINDOW, dh), BF16)
    vwint_ref[0:nwb] = jnp.zeros((nwb, vr, tk), BF16)
    ones_rows = jnp.where(lax.broadcasted_iota(jnp.int32, (vr - dh, tk), 0) == 0, 1.0, 0.0).astype(BF16)

    def prep(blk):
        r0 = pl.multiple_of(blk * tk, tk)
        cos_t = cos_ref[0, pl.ds(r0, tk), :]
        sin_t = sin_ref[0, pl.ds(r0, tk), :]
        cq = jnp.concatenate([cos_t] * (qw // LANE), axis=1)
        sq = jnp.concatenate([sin_t] * (qw // LANE), axis=1)
        qr = _rope_apply(q_ref[pl.ds(r0, tk), :], cq, sq, perm_ref[...]) * scale
        qt_ref[blk] = qr.T.astype(BF16)
        ck = jnp.where(lane < dh, cos_t, 1.0)
        sk = jnp.where(lane < dh, sin_t, 0.0)
        kv = kv_ref[pl.ds(r0, tk), :]
        perm_kv = perm_ref[0:LANE, 0:LANE]
        csrc_ref[pl.ds(r0, tk), :] = _rope_apply(kv[:, 0:LANE], ck, sk, perm_kv)
        kv_s = _rope_apply(kv[:, LANE:2 * LANE], ck, sk, perm_kv)
        kslc_ref[pl.ds(r0, tk), :] = kv_s[:, :dh].astype(BF16)
        vslct_ref[blk] = jnp.concatenate([kv_s.T[dh:, :].astype(BF16), ones_rows], axis=0)
        kv_w = _rope_apply(kv[:, 2 * LANE:3 * LANE], ck, sk, perm_kv)
        kwin_ref[pl.ds(pl.multiple_of(WINDOW + r0, tk), tk), :] = kv_w[:, :dh].astype(BF16)
        vwint_ref[nwb + blk] = jnp.concatenate([kv_w.T[dh:, :].astype(BF16), ones_rows], axis=0)

    def prep_blocks(i, carry):
        for u in range(NSA_PREP_UNROLL):
            prep(i * NSA_PREP_UNROLL + u)
        return carry

    lax.fori_loop(0, seq // (tk * NSA_PREP_UNROLL), prep_blocks, 0)

    for l in range(L_CMP):
        ccat_ref[:, l * LANE:(l + 1) * LANE] = (
            csrc_ref[pl.ds(l, nc, stride=STRIDE_CMP), :] + pe_ref[l:l + 1, :])
    kvc = _dot(ccat_ref[...], wc_ref[...])
    kc_ref[...] = kvc[:, :dh].astype(BF16)
    vct_ref[...] = kvc.T[dh:, :].astype(BF16)

    n_sub = lax.broadcasted_iota(jnp.int32, (nc, 1), 0)
    j_sub = lax.broadcasted_iota(jnp.int32, (ns, 1), 0)
    k_sub = lax.broadcasted_iota(jnp.int32, (kb, 1), 0)
    c_sub = lax.broadcasted_iota(jnp.int32, (band, 1), 0)
    q_lane = lax.broadcasted_iota(jnp.int32, (1, tq), 1)
    win_static = (c_sub - WINDOW <= q_lane) & (c_sub > q_lane)
    heads = [slice(h * tq, (h + 1) * tq) for h in range(hpg)]

    def q_tile(i, carry):
        r0 = pl.multiple_of(i * tq, tq)
        b0 = i * (tq // tk)
        qblk = [qt_ref[b0 + jb] for jb in range(tq // tk)]
        qt4 = jnp.concatenate([qb[h * dh:(h + 1) * dh, :] for h in range(hpg) for qb in qblk],
                              axis=1)
        t_lane = r0 + q_lane

        s = _dot(kc_ref[...], qt4)
        cmask = (n_sub * STRIDE_CMP + (L_CMP - 1)) <= t_lane
        ps = [_masked_softmax0(s[:, sl], cmask) for sl in heads]
        o_cmp = _dot(vct_ref[...], jnp.concatenate([p.astype(BF16) for p in ps], axis=1))
        psum = ps[0]
        for p in ps[1:]:
            psum = psum + p

        imp = _dot(ovl_ref[...], psum, hi=True)
        cur = t_lane // L_SLC
        causal_blk = j_sub <= cur
        forced = (j_sub == 0) | (j_sub == cur) | (j_sub == cur - 1)
        impv = jnp.where(forced, 1e9, jnp.where(causal_blk, imp, -1e9))
        rank = jnp.zeros((ns, tq), F32)
        for b in range(ns):
            row = impv[b:b + 1, :]
            beats = (row > impv) | ((row == impv) & (b < j_sub))
            rank = rank + jnp.where(beats, 1.0, 0.0)
        sel = jnp.where((rank < n_top) & causal_blk, 1.0, 0.0)
        sel_ref[...] = sel

        s = _dot(kwin_ref[pl.ds(r0, band), :], qt4)
        wmask = win_static & (c_sub >= WINDOW - r0)
        wmb = jnp.where(wmask, 1.0, 0.0).astype(BF16)
        ps = []
        for sl in heads:
            sh = jnp.where(wmask, s[:, sl], -1e30)
            ps.append(_exp_masked_bf16(sh, jnp.max(sh, axis=0, keepdims=True), wmb))
        p4 = jnp.concatenate(ps, axis=1)
        ow = _dot(vwint_ref[b0], p4[0:tk, :])
        for jb in range(1, band // tk):
            ow = ow + _dot(vwint_ref[b0 + jb], p4[jb * tk:(jb + 1) * tk, :])
        o_win = ow[0:dh] * (1.0 / jnp.maximum(ow[dh:dh + 1], 1e-30))

        gt = _dot(gsel_ref[0], jax.nn.sigmoid(sm_ref[pl.ds(r0, tq), :]), NT, hi=True)
        gate = lambda br: jnp.concatenate(
            [jnp.broadcast_to(gt[br * hpg + h:br * hpg + h + 1], (dh, tq)) for h in range(hpg)], axis=1)
        o_cw = gate(0) * o_cmp + gate(2) * o_win
        g_slc = gate(1)

        def kv_step(j, st):
            m, acc = st
            k0 = pl.multiple_of(j * kb, kb)
            s = _dot(kslc_ref[pl.ds(k0, kb), :], qt4)
            sel_rows = sel_ref[pl.ds(pl.multiple_of(j * nsb, nsb), nsb), :]
            selm = jnp.concatenate([jnp.broadcast_to(sel_rows[b:b + 1, :], (L_SLC, tq)) for b in range(nsb)], axis=0)
            msk = (selm > 0.5) & (k0 + k_sub <= t_lane)
            mb = jnp.where(msk, 1.0, 0.0).astype(BF16)
            ms, als, ps = [], [], []
            for sl in heads:
                sh = jnp.where(msk, s[:, sl], -1e30)
                mn = jnp.maximum(m[:, sl], jnp.max(sh, axis=0, keepdims=True))
                ms.append(mn)
                als.append(jnp.exp2(m[:, sl] - mn))
                ps.append(_exp_masked_bf16(sh, mn, mb))
            p4 = jnp.concatenate(ps, axis=1)
            pv = _dot(vslct_ref[j * (kb // tk)], p4[0:tk, :])
            for hf in range(1, kb // tk):
                pv = pv + _dot(vslct_ref[j * (kb // tk) + hf], p4[hf * tk:(hf + 1) * tk, :])
            return jnp.concatenate(ms, axis=1), jnp.concatenate(als, axis=1) * acc + pv

        init = (jnp.full((1, hpg * tq), -1e30, F32), jnp.zeros((vr, hpg * tq), F32))
        _, acc_s = lax.fori_loop(0, (r0 + tq + kb - 1) // kb, kv_step, init)
        o_slc = acc_s[0:dh] * (1.0 / jnp.maximum(acc_s[dh:dh + 1], 1e-30))

        out_t = o_cw + g_slc * o_slc
        out_t = jnp.concatenate([out_t[:, sl] for sl in heads], axis=0)
        o_ref[pl.ds(r0, tq), :] = out_t.T.astype(o_ref.dtype)
        return carry

    lax.fori_loop(0, seq // tq, q_tile, 0)


def _nsa_constants(seq):
    nc = seq // STRIDE_CMP
    ns = seq // L_SLC
    cs = np.arange(nc) * STRIDE_CMP
    js = np.arange(ns) * L_SLC
    ovl_t = ((cs[None, :] < js[:, None] + L_SLC) & (cs[None, :] + L_CMP > js[:, None])).astype(np.float32)
    gsel = np.zeros((NSA_G, 16, LANE), np.float32)
    for g in range(NSA_G):
        for h in range(NSA_HPG):
            for j in range(3):
                gsel[g, j * NSA_HPG + h, SM_GATE + (g * NSA_HPG + h) * 3 + j] = 1.0
    qw = NSA_HPG * NSA_DH
    half = ROPE_DIM // 2
    perm = np.zeros((qw, qw), np.float32)
    for l in range(qw):
        d = l % NSA_DH
        if d < ROPE_DIM:
            perm[l + half if d < half else l - half, l] = 1.0
    return jnp.asarray(ovl_t), jnp.asarray(gsel), jnp.asarray(perm, dtype=BF16)


def _nsa(proj, q_blk, kv_blk, sm_blk, bsz, seq, cos_t, sin_t, pe_k, pe_v, wk_cmp, wv_cmp, consts):
    t = bsz * seq
    dh = NSA_DH
    qw = NSA_HPG * dh
    tq, tk = NSA_TQ, NSA_TK
    nc = seq // STRIDE_CMP
    ns = seq // L_SLC
    ovl_t, gsel, perm = consts
    pe = jnp.concatenate([pe_k, pe_v], axis=1)
    zero = jnp.zeros((L_CMP, dh, dh), F32)
    wc = jnp.concatenate([jnp.concatenate([wk_cmp, zero], axis=2),
                          jnp.concatenate([zero, wv_cmp], axis=2)], axis=1)
    wc = wc.reshape(L_CMP * LANE, LANE).astype(BF16)
    kern = functools.partial(_nsa_kernel, seq=seq)
    full = lambda shape: pl.BlockSpec(shape, lambda b, g: tuple(0 for _ in shape))
    return pl.pallas_call(
        kern,
        grid=(bsz, NSA_G),
        in_specs=[pl.BlockSpec((seq, qw), lambda b, g: (b, q_blk + g)),
                  pl.BlockSpec((seq, 3 * LANE), lambda b, g: (b, kv_blk + g)),
                  pl.BlockSpec((seq, LANE), lambda b, g: (b, sm_blk)),
                  pl.BlockSpec((1, seq, LANE), lambda b, g: (b, 0, 0)),
                  pl.BlockSpec((1, seq, LANE), lambda b, g: (b, 0, 0)),
                  full((L_CMP, LANE)),
                  full((L_CMP * LANE, LANE)),
                  full((ns, nc)),
                  pl.BlockSpec((1, 16, LANE), lambda b, g: (g, 0, 0)),
                  full((qw, qw))],
        out_specs=pl.BlockSpec((seq, qw), lambda b, g: (b, g)),
        out_shape=jax.ShapeDtypeStruct((t, NSA_G * qw), BF16),
        scratch_shapes=[pltpu.VMEM((seq // tk, qw, tk), BF16),
                        pltpu.VMEM((seq + L_CMP, LANE), F32),
                        pltpu.VMEM((nc, L_CMP * LANE), F32),
                        pltpu.VMEM((nc, dh), BF16),
                        pltpu.VMEM((dh, nc), BF16),
                        pltpu.VMEM((seq, dh), BF16),
                        pltpu.VMEM((seq // tk, NSA_VR, tk), BF16),
                        pltpu.VMEM((WINDOW + seq, dh), BF16),
                        pltpu.VMEM(((WINDOW + seq) // tk, NSA_VR, tk), BF16),
                        pltpu.VMEM((ns, tq), F32)],
        compiler_params=_params("parallel", "parallel"),
        name="nsa",
    )(proj, proj, proj, cos_t, sin_t, pe, wc, ovl_t, gsel, perm)


def _dot3(a, b):
    a_hi = a.astype(BF16)
    a_lo = (a - a_hi.astype(F32)).astype(BF16)
    b_hi = b.astype(BF16)
    b_lo = (b - b_hi.astype(F32)).astype(BF16)
    d = lambda x, y: jnp.dot(x, y, preferred_element_type=F32)
    return d(a_hi, b_hi) + (d(a_lo, b_hi) + d(a_hi, b_lo))


def _unit_lower_solve(lmats, rhss):
    c, width = rhss[0].shape
    bs = GDN_SOLVE_BLOCK
    done = [[] for _ in rhss]
    for b in range(c // bs):
        r0 = b * bs
        xbs = []
        for lmat, rhs, dn in zip(lmats, rhss, done):
            xb = rhs[r0:r0 + bs]
            if b:
                partial = jnp.concatenate(dn + [jnp.zeros((c - r0, width), F32)], axis=0)
                xb = xb - _dot3(lmat[r0:r0 + bs, :], partial)
            xbs.append(xb)
        for lmat, xb, dn in zip(lmats, xbs, done):
            lb = lmat[r0:r0 + bs, r0:r0 + bs]
            groups = [xb[8 * g:8 * g + 8] for g in range(bs // 8)]
            for j in range(bs - 1):
                row = groups[j // 8][j % 8:j % 8 + 1, :]
                for g in range(j // 8, bs // 8):
                    if 8 * g + 7 > j:
                        groups[g] = groups[g] - lb[8 * g:8 * g + 8, j:j + 1] * row
            dn.append(jnp.concatenate(groups, axis=0))
    return [jnp.concatenate(dn, axis=0) for dn in done]


def _gdn_kernel(qkv_ref, z_ref, sm_ref, cw_ref, alog_ref, dtb_ref, ng_ref, o_ref, buf_ref, xc_ref, st_ref,
                *, k_len):
    c = GDN_CHUNK
    rows = c * GDN_STEP_CHUNKS
    dk, dv = GDN_DK, GDN_DV

    @pl.when(pl.program_id(1) == 0)
    def _():
        buf_ref[0:GDN_HALO, :] = jnp.zeros((GDN_HALO, GDN_QKV), F32)
        st_ref[...] = jnp.zeros(st_ref.shape, F32)

    x = qkv_ref[...]
    buf_ref[GDN_HALO:GDN_HALO + rows, :] = x
    base = GDN_HALO - (k_len - 1)
    acc = jnp.zeros((rows, GDN_QKV), F32)
    for k in range(k_len):
        acc = acc + cw_ref[k:k + 1, :] * buf_ref[base + k:base + k + rows, :]
    buf_ref[0:GDN_HALO, :] = x[rows - GDN_HALO:rows, :]
    xc_ref[...] = acc * jax.nn.sigmoid(acc)

    sm = sm_ref[...]
    beta_all = jax.nn.sigmoid(sm)
    g_all = -jnp.exp(alog_ref[...]) * jax.nn.softplus(sm + dtb_ref[...])
    ri = lax.broadcasted_iota(jnp.int32, (rows, rows), 0)
    rj = lax.broadcasted_iota(jnp.int32, (rows, rows), 1)
    same_chunk_incl = (ri >= rj) & (ri // c == rj // c)
    gc_all = _dot(jnp.where(same_chunk_incl, 1.0, 0.0), g_all, hi=True)
    gc_rows = gc_all.T
    ii = lax.broadcasted_iota(jnp.int32, (c, c), 0)
    jj = lax.broadcasted_iota(jnp.int32, (c, c), 1)
    incl = ii >= jj
    strict = ii > jj

    pre, lmats, rhss = [], [], []
    for n in range(GDN_STEP_CHUNKS):
        rs = slice(n * c, (n + 1) * c)
        for h in range(GDN_H):
            gcol = gc_all[rs, SM_A + h:SM_A + h + 1]
            bcol = beta_all[rs, SM_BETA + h:SM_BETA + h + 1]
            grow = gc_rows[SM_A + h:SM_A + h + 1, rs]
            decay = jnp.exp(jnp.where(incl, gcol - grow, -jnp.inf))
            q = xc_ref[rs, h * dk:(h + 1) * dk]
            k = xc_ref[rs, GDN_H * dk + h * dk:GDN_H * dk + (h + 1) * dk]
            v = xc_ref[rs, 2 * GDN_H * dk + h * dv:2 * GDN_H * dk + (h + 1) * dv]
            q = q * lax.rsqrt(jnp.sum(q * q, axis=-1, keepdims=True) + EPS) * (dk ** -0.5)
            k = k * lax.rsqrt(jnp.sum(k * k, axis=-1, keepdims=True) + EPS)
            eg = jnp.exp(gcol)
            lmat = jnp.where(strict, bcol * _dot(k, k, NT) * decay, 0.0)
            qk = jnp.where(incl, _dot(q, k, NT) * decay, 0.0)
            glast = gcol[c - 1:c, :]
            lmats.append(lmat)
            rhss.append(jnp.concatenate([k * (bcol * eg), v * bcol], axis=1))
            pre.append((q * eg, k * jnp.exp(glast - gcol), qk, jnp.exp(glast)))
    sols = _unit_lower_solve(lmats, rhss)

    z = z_ref[...]
    for n in range(GDN_STEP_CHUNKS):
        rs = slice(n * c, (n + 1) * c)
        idx = range(n * GDN_H, (n + 1) * GDN_H)
        states = [st_ref[h] for h in range(GDN_H)]
        v_news = [sols[i][:, dk:] - _dot(sols[i][:, :dk], st) for i, st in zip(idx, states)]
        o_state = [_dot(pre[i][0], st) for i, st in zip(idx, states)]
        o_intra = [_dot(pre[i][2], vn) for i, vn in zip(idx, v_news)]
        s_add = [_dot(pre[i][1], vn, TN) for i, vn in zip(idx, v_news)]
        for h, i in enumerate(idx):
            st_ref[h] = pre[i][3] * states[h] + s_add[h]
            o = o_state[h] + o_intra[h]
            o = o * lax.rsqrt(jnp.mean(o * o, axis=-1, keepdims=True) + EPS) * ng_ref[...]
            zh = z[rs, h * dv:(h + 1) * dv]
            o_ref[rs, h * dv:(h + 1) * dv] = (o * (zh * jax.nn.sigmoid(zh))).astype(o_ref.dtype)


def _gdn(proj, qkv_blk, z_blk, sm_blk, bsz, seq, conv_w, a_log, dt_bias, norm_g):
    t = bsz * seq
    c = GDN_CHUNK * GDN_STEP_CHUNKS
    nch = seq // c
    k_len = conv_w.shape[0]
    alog = jnp.zeros((1, LANE), F32).at[0, SM_A:SM_A + GDN_H].set(a_log)
    dtb = jnp.zeros((1, LANE), F32).at[0, SM_A:SM_A + GDN_H].set(dt_bias)
    kern = functools.partial(_gdn_kernel, k_len=k_len)
    full = lambda shape: pl.BlockSpec(shape, lambda b, n: tuple(0 for _ in shape))
    return pl.pallas_call(
        kern,
        grid=(bsz, nch),
        in_specs=[pl.BlockSpec((c, GDN_QKV), lambda b, n: (b * nch + n, qkv_blk)),
                  pl.BlockSpec((c, GDN_WIDTH), lambda b, n: (b * nch + n, z_blk)),
                  pl.BlockSpec((c, LANE), lambda b, n: (b * nch + n, sm_blk)),
                  full((k_len, GDN_QKV)), full((1, LANE)), full((1, LANE)), full((1, GDN_DV))],
        out_specs=pl.BlockSpec((c, GDN_WIDTH), lambda b, n: (b * nch + n, 0)),
        out_shape=jax.ShapeDtypeStruct((t, GDN_WIDTH), BF16),
        scratch_shapes=[pltpu.VMEM((GDN_HALO + c, GDN_QKV), F32),
                        pltpu.VMEM((c, GDN_QKV), F32),
                        pltpu.VMEM((GDN_H, GDN_DK, GDN_DV), F32)],
        compiler_params=_params("parallel", "arbitrary"),
        name="gdn",
    )(proj, proj, proj, conv_w, alog, dtb, norm_g.reshape(1, -1))


def _merge_kernel(x_ref, mod_ref, yc_ref, yn_ref, yg_ref, gm_ref, wc_ref, wn_ref, wg_ref, wo_ref, o_ref):
    d = x_ref.shape[1]
    gm = jax.nn.sigmoid(gm_ref[...])
    merged = (gm[:, 0:d] * jnp.dot(yc_ref[...], wc_ref[...], preferred_element_type=F32)
              + gm[:, d:2 * d] * jnp.dot(yn_ref[...], wn_ref[...], preferred_element_type=F32)
              + gm[:, 2 * d:3 * d] * jnp.dot(yg_ref[...], wg_ref[...], preferred_element_type=F32))
    gt = mod_ref[0][2:3]
    o_ref[...] = x_ref[...] + gt * _dot(merged, wo_ref[...])


def _merge(x2, mod_l, y_conv, y_nsa, y_gdn, proj, gm_blk, w_c, w_n, w_g, w_o, seq, tm=512):
    t, d = x2.shape
    per_b = seq // tm
    row = lambda w: pl.BlockSpec((tm, w), lambda i: (i, 0))
    wspec = lambda w: pl.BlockSpec(w.shape, lambda i: (0, 0))
    return pl.pallas_call(
        _merge_kernel,
        grid=(t // tm,),
        in_specs=[row(d),
                  pl.BlockSpec((1, N_MOD, d), lambda i: (i // per_b, 0, 0)),
                  row(y_conv.shape[1]), row(y_nsa.shape[1]), row(y_gdn.shape[1]),
                  pl.BlockSpec((tm, 3 * d), lambda i: (i, gm_blk)),
                  wspec(w_c), wspec(w_n), wspec(w_g), wspec(w_o)],
        out_specs=row(d),
        out_shape=jax.ShapeDtypeStruct((t, d), F32),
        compiler_params=_params("parallel"),
        name="merge",
    )(x2, mod_l, y_conv, y_nsa, y_gdn, proj, w_c, w_n, w_g, w_o)


def _ffn_kernel(x_ref, mod_ref, g_ref, wg_ref, wu_ref, wo_ref, fg_ref, o_ref, h_ref, acc_ref, *, final):
    j = pl.program_id(1)

    @pl.when(j == 0)
    def _():
        m = mod_ref[0]
        h_ref[...] = _norm_mod(x_ref[...], g_ref[...], m[3:4], m[4:5]).astype(BF16)
        acc_ref[...] = jnp.zeros(acc_ref.shape, F32)

    h = h_ref[...]
    gate = jnp.dot(h, wg_ref[...], preferred_element_type=F32)
    up = jnp.dot(h, wu_ref[...], preferred_element_type=F32)
    act = (gate * jax.nn.sigmoid(gate)) * up
    acc_ref[...] += _dot(act, wo_ref[...])

    @pl.when(j == pl.num_programs(1) - 1)
    def _():
        y = x_ref[...] + mod_ref[0][5:6] * acc_ref[...]
        if final:
            y = y * lax.rsqrt(jnp.mean(y * y, axis=-1, keepdims=True) + EPS) * fg_ref[...]
        o_ref[...] = y


def _ffn(x2, mod_l, g, w_in, w_out, final_g, seq, final, tm=512, th=1408):
    t, d = x2.shape
    hid = w_out.shape[0]
    per_b = seq // tm
    nh = hid // th
    kern = functools.partial(_ffn_kernel, final=final)
    return pl.pallas_call(
        kern,
        grid=(t // tm, nh),
        in_specs=[pl.BlockSpec((tm, d), lambda i, j: (i, 0)),
                  pl.BlockSpec((1, N_MOD, d), lambda i, j: (i // per_b, 0, 0)),
                  pl.BlockSpec((1, d), lambda i, j: (0, 0)),
                  pl.BlockSpec((d, th), lambda i, j: (0, j)),
                  pl.BlockSpec((d, th), lambda i, j: (0, nh + j)),
                  pl.BlockSpec((th, d), lambda i, j: (j, 0)),
                  pl.BlockSpec((1, d), lambda i, j: (0, 0))],
        out_specs=pl.BlockSpec((tm, d), lambda i, j: (i, 0)),
        out_shape=jax.ShapeDtypeStruct((t, d), F32),
        scratch_shapes=[pltpu.VMEM((tm, d), BF16), pltpu.VMEM((tm, d), F32)],
        compiler_params=_params("parallel", "arbitrary"),
        name="ffn",
    )(x2, mod_l, g, w_in, w_in, w_out, final_g)


def _in_layout(d):
    segs = {}
    off = 0
    for name, width in (("gm", 3 * d), ("conv", 2 * C_CONV), ("q", NSA_WIDTH), ("qkv", GDN_QKV),
                        ("z", GDN_WIDTH), ("small", LANE), ("pad", LANE),
                        ("kv", 6 * NSA_G * NSA_DH)):
        segs[name] = off
        off += width
    return segs, off


def _rearrange_in_proj(w_in, b_in, d):
    splits = (2 * C_CONV, NSA_WIDTH, 6 * NSA_G * NSA_DH, 3 * NSA_HEADS, GDN_QKV, GDN_WIDTH, GDN_H, GDN_H, 3 * d)
    offs = np.concatenate([[0], np.cumsum(splits)])
    names = ("conv", "q", "kv", "gnsa", "qkv", "z", "beta", "a", "gm")
    segs, total = _in_layout(d)

    def arrange(a):
        src = {n: a[..., int(offs[i]):int(offs[i + 1])] for i, n in enumerate(names)}
        lead = a.shape[:-1]
        kv = jnp.swapaxes(src["kv"].reshape(lead + (6, NSA_G, NSA_DH)), -3, -2).reshape(lead + (-1,))
        small_pad = jnp.zeros(lead + (LANE - 3 * NSA_HEADS - 2 * GDN_H,), a.dtype)
        pad = jnp.zeros(lead + (LANE,), a.dtype)
        out = jnp.concatenate([src["gm"], src["conv"], src["q"], src["qkv"], src["z"],
                               src["gnsa"], src["beta"], src["a"], small_pad, pad, kv], axis=-1)
        assert out.shape[-1] == total
        return out

    return arrange(w_in).astype(BF16), arrange(b_in)[:, None, :], segs


def kernel(x, c, positions, ada_w, ada_b, norm_mix_g, norm_ffn_g, w_in, b_in, conv_dw_w, conv_dw_b, conv_ln_g, conv_ln_b, nsa_pe_k, nsa_pe_v, nsa_wk_cmp, nsa_wv_cmp, gdn_conv_w, gdn_a_log, gdn_dt_bias, gdn_norm_g, w_up_conv, w_up_nsa, w_up_gdn, w_o, ffn_w_in, ffn_w_out, final_norm_g):
    bsz, seq, d = x.shape
    depth = ada_w.shape[0]
    t = bsz * seq

    mod = _modulation(c, ada_w, ada_b).reshape(depth, bsz, N_MOD, d)
    cos_t, sin_t = _rope_tables(positions)
    nsa_consts = _nsa_constants(seq)
    w_in_r, b_in_r, segs = _rearrange_in_proj(w_in, b_in, d)
    blk = lambda name, width: segs[name] // width
    qw = NSA_HPG * NSA_DH

    x2 = x.reshape(t, d)
    for l in range(depth):
        proj = _in_projection(x2, mod[l], norm_mix_g[l].reshape(1, d), w_in_r[l], b_in_r[l], seq)
        y_conv = _conformer_conv(proj, blk("conv", 2 * C_CONV), bsz, seq,
                                 conv_dw_w[l], conv_dw_b[l], conv_ln_g[l], conv_ln_b[l])
        y_nsa = _nsa(proj, blk("q", qw), blk("kv", 3 * LANE), blk("small", LANE), bsz, seq, cos_t, sin_t,
                     nsa_pe_k[l], nsa_pe_v[l], nsa_wk_cmp[l], nsa_wv_cmp[l], nsa_consts)
        y_gdn = _gdn(proj, blk("qkv", GDN_QKV), blk("z", GDN_WIDTH), blk("small", LANE), bsz, seq,
                     gdn_conv_w[l], gdn_a_log[l], gdn_dt_bias[l], gdn_norm_g[l])
        x2 = _merge(x2, mod[l], y_conv, y_nsa, y_gdn, proj, blk("gm", 3 * d),
                    w_up_conv[l].astype(BF16), w_up_nsa[l].astype(BF16), w_up_gdn[l].astype(BF16),
                    w_o[l].astype(BF16), seq)
        x2 = _ffn(x2, mod[l], norm_ffn_g[l].reshape(1, d), ffn_w_in[l].astype(BF16),
                  ffn_w_out[l].astype(BF16), final_norm_g.reshape(1, d), seq, final=(l == depth - 1))
    return x2.reshape(bsz, seq, d)
```

```python
import functools

import numpy as np
import jax
import jax.numpy as jnp
from jax import lax
from jax.experimental import pallas as pl
from jax.experimental.pallas import tpu as pltpu

F32 = jnp.float32
BF16 = jnp.bfloat16
HI = lax.Precision.HIGHEST
NT = (((1,), (1,)), ((), ()))
TN = (((0,), (0,)), ((), ()))

EPS = 1e-6
LOG2_E = 1.4426950408889634
N_MOD = 6
C_CONV = 512
CONV_HALO = 32
NSA_HEADS = 8
NSA_G = 2
NSA_HPG = NSA_HEADS // NSA_G
NSA_DH = 64
NSA_WIDTH = NSA_HEADS * NSA_DH
ROPE_DIM = NSA_DH // 4
ROPE_THETA = 500000.0
L_CMP = 32
STRIDE_CMP = 16
L_SLC = 64
N_SLC_TOP = 8
WINDOW = 512
NSA_TQ = 512
NSA_TK = 128
NSA_PREP_UNROLL = 4
NSA_KB = 512
NSA_VR = NSA_DH + 16
GDN_H = 4
GDN_DK = 128
GDN_DV = 128
GDN_WIDTH = GDN_H * GDN_DV
GDN_QKV = GDN_H * (2 * GDN_DK + GDN_DV)
GDN_CHUNK = 64
GDN_HALO = 8
GDN_STEP_CHUNKS = 4
GDN_SOLVE_BLOCK = 8
SM_GATE = 0
SM_BETA = 24
SM_A = 28
LANE = 128

VMEM_LIMIT = 56 * 1024 * 1024


def _dot(a, b, dims=None, hi=False):
    if dims is None:
        dims = (((a.ndim - 1,), (0,)), ((), ()))
    if hi:
        return lax.dot_general(a.astype(F32), b.astype(F32), dims, precision=HI,
                               preferred_element_type=F32)
    return lax.dot_general(a.astype(BF16), b.astype(BF16), dims, preferred_element_type=F32)


def _masked_softmax0(s, mask):
    s = jnp.where(mask, s, -1e30)
    m = jnp.max(s, axis=0, keepdims=True)
    p = jnp.where(mask, jnp.exp2(s - m), 0.0)
    return p * (1.0 / jnp.maximum(jnp.sum(p, axis=0, keepdims=True), 1e-30))


def _exp_masked_bf16(sh, m, mb):
    return jnp.exp2((sh - m).astype(BF16)) * mb


def _params(*sem):
    return pltpu.CompilerParams(dimension_semantics=sem, vmem_limit_bytes=VMEM_LIMIT)


def _mod_kernel(c_ref, w_ref, b_ref, o_ref):
    c = c_ref[...]
    ca = c * jax.nn.sigmoid(c)
    o_ref[0] = _dot(ca, w_ref[0], hi=True) + b_ref[0]


def _modulation(c, ada_w, ada_b):
    depth, d, n = ada_w.shape
    bsz = c.shape[0]
    tn = 1536
    return pl.pallas_call(
        _mod_kernel,
        grid=(depth, n // tn),
        in_specs=[pl.BlockSpec((bsz, d), lambda l, j: (0, 0)),
                  pl.BlockSpec((1, d, tn), lambda l, j: (l, 0, j)),
                  pl.BlockSpec((1, 1, tn), lambda l, j: (l, 0, j))],
        out_specs=pl.BlockSpec((1, bsz, tn), lambda l, j: (l, 0, j)),
        out_shape=jax.ShapeDtypeStruct((depth, bsz, n), F32),
        compiler_params=_params("parallel", "parallel"),
        name="adaln_mod",
    )(c, ada_w, ada_b.reshape(depth, 1, n))


def _norm_mod(x, g, sh, sc):
    y = x * lax.rsqrt(jnp.mean(x * x, axis=-1, keepdims=True) + EPS)
    return (y * g) * (1.0 + sc) + sh


def _inproj_kernel(x_ref, mod_ref, g_ref, w_ref, b_ref, o_ref, h_ref):
    @pl.when(pl.program_id(1) == 0)
    def _():
        m = mod_ref[0]
        h_ref[...] = _norm_mod(x_ref[...], g_ref[...], m[0:1], m[1:2]).astype(BF16)

    o_ref[...] = jnp.dot(h_ref[...], w_ref[0], preferred_element_type=F32) + b_ref[0]


def _in_projection(x2, mod_l, g, w, b, layer, seq, tm=1024, tn=1536):
    t, d = x2.shape
    n = w.shape[2]
    per_b = seq // tm
    return pl.pallas_call(
        _inproj_kernel,
        grid=(t // tm, n // tn),
        in_specs=[pl.BlockSpec((tm, d), lambda i, j: (i, 0)),
                  pl.BlockSpec((1, N_MOD, d), lambda i, j: (i // per_b, 0, 0)),
                  pl.BlockSpec((1, d), lambda i, j: (0, 0)),
                  pl.BlockSpec((1, d, tn), lambda i, j: (layer, 0, j)),
                  pl.BlockSpec((1, 1, tn), lambda i, j: (layer, 0, j))],
        out_specs=pl.BlockSpec((tm, tn), lambda i, j: (i, j)),
        out_shape=jax.ShapeDtypeStruct((t, n), F32),
        scratch_shapes=[pltpu.VMEM((tm, d), BF16)],
        compiler_params=_params("parallel", "arbitrary"),
        name="in_proj",
    )(x2, mod_l, g, w, b)


def _conv_kernel(u_ref, w_ref, b_ref, lg_ref, lb_ref, o_ref, buf_ref, *, ts, k_len, rc):
    c = C_CONV

    @pl.when(pl.program_id(1) == 0)
    def _():
        buf_ref[0:CONV_HALO, :] = jnp.zeros((CONV_HALO, c), F32)

    @pl.when(pl.program_id(1) > 0)
    def _():
        buf_ref[0:CONV_HALO, :] = buf_ref[ts:ts + CONV_HALO, :]

    u = u_ref[...]
    buf_ref[CONV_HALO:CONV_HALO + ts, :] = u[:, :c] * jax.nn.sigmoid(u[:, c:])
    base = CONV_HALO - (k_len - 1)

    def chunk(ci, carry):
        r0 = pl.multiple_of(ci * rc, rc)
        win = buf_ref[pl.ds(r0, rc + CONV_HALO), :]
        acc = jnp.zeros((rc // 8, 8, c), F32)
        for ph in range(8):
            taps = [k for k in range(k_len) if (base + k) % 8 == ph]
            if not taps:
                continue
            span = max(base + k for k in taps) - ph + rc
            shifted = win[ph:ph + span, :]
            for k in taps:
                off = base + k - ph
                acc = acc + w_ref[k][None] * shifted[off:off + rc, :].reshape(rc // 8, 8, c)
        y = acc.reshape(rc, c) + b_ref[...]
        mu = jnp.mean(y, axis=-1, keepdims=True)
        var = jnp.mean(jnp.square(y - mu), axis=-1, keepdims=True)
        y = (y - mu) * lax.rsqrt(var + EPS) * lg_ref[...] + lb_ref[...]
        o_ref[pl.ds(r0, rc), :] = (y * jax.nn.sigmoid(y)).astype(o_ref.dtype)
        return carry

    lax.fori_loop(0, ts // rc, chunk, 0)


def _conformer_conv(proj, col_blk, bsz, seq, dw_w, dw_b, ln_g, ln_b, ts=512, rc=128):
    t = bsz * seq
    k_len = dw_w.shape[0]
    per_b = seq // ts
    kern = functools.partial(_conv_kernel, ts=ts, k_len=k_len, rc=rc)
    vec = lambda: pl.BlockSpec((1, C_CONV), lambda b, s: (0, 0))
    return pl.pallas_call(
        kern,
        grid=(bsz, per_b),
        in_specs=[pl.BlockSpec((ts, 2 * C_CONV), lambda b, s: (b * per_b + s, col_blk)),
                  pl.BlockSpec((k_len, 8, C_CONV), lambda b, s: (0, 0, 0)),
                  vec(), vec(), vec()],
        out_specs=pl.BlockSpec((ts, C_CONV), lambda b, s: (b * per_b + s, 0)),
        out_shape=jax.ShapeDtypeStruct((t, C_CONV), BF16),
        scratch_shapes=[pltpu.VMEM((CONV_HALO + ts, C_CONV), F32)],
        compiler_params=_params("parallel", "arbitrary"),
        name="conformer_conv",
    )(proj, jnp.broadcast_to(dw_w[:, None, :], (k_len, 8, C_CONV)),
      dw_b.reshape(1, -1), ln_g.reshape(1, -1), ln_b.reshape(1, -1))


def _rope_table_kernel(pos_ref, inv_ref, sign_ref, cos_ref, sin_ref):
    ang = pos_ref[0].astype(F32) * inv_ref[...]
    cos_ref[0] = jnp.cos(ang)
    sin_ref[0] = jnp.sin(ang) * sign_ref[...]


def _rope_tables(positions):
    bsz, seq = positions.shape
    half = ROPE_DIM // 2
    inv = ROPE_THETA ** (-jnp.arange(half, dtype=F32) * 2.0 / ROPE_DIM)
    d = np.arange(LANE) % NSA_DH
    inv_lane = jnp.where(jnp.asarray(d < ROPE_DIM), inv[d % half], 0.0).reshape(1, LANE)
    sign_lane = jnp.asarray(np.where(d < half, -1.0, np.where(d < ROPE_DIM, 1.0, 0.0)),
                            dtype=F32).reshape(1, LANE)
    out = jax.ShapeDtypeStruct((bsz, seq, LANE), F32)
    return pl.pallas_call(
        _rope_table_kernel,
        grid=(bsz,),
        in_specs=[pl.BlockSpec((1, seq, 1), lambda b: (b, 0, 0)),
                  pl.BlockSpec((1, LANE), lambda b: (0, 0)),
                  pl.BlockSpec((1, LANE), lambda b: (0, 0))],
        out_specs=[pl.BlockSpec((1, seq, LANE), lambda b: (b, 0, 0))] * 2,
        out_shape=[out, out],
        compiler_params=_params("parallel"),
        name="rope_tables",
    )(positions.reshape(bsz, seq, 1), inv_lane, sign_lane)


def _rope_apply(x, cos_t, sin_t, perm):
    partner = jnp.dot(x.astype(BF16), perm, preferred_element_type=F32)
    return x * cos_t + partner * sin_t


def _nsa_kernel(q_ref, kv_ref, sm_ref, cos_ref, sin_ref, pe_ref, wc_ref, ovl_ref, gsel_ref, perm_ref,
                o_ref,
                qt_ref, csrc_ref, ccat_ref, kc_ref, vct_ref, kslc_ref, vslct_ref, kwin_ref, vwint_ref,
                sel_ref, *, seq):
    tq, tk, kb, dh, hpg, vr = NSA_TQ, NSA_TK, NSA_KB, NSA_DH, NSA_HPG, NSA_VR
    nc = seq // STRIDE_CMP
    ns = seq // L_SLC
    nwb = WINDOW // tk
    band = WINDOW + tq
    scale = dh ** -0.5 * LOG2_E
    qw = hpg * dh
    n_top = float(min(N_SLC_TOP, ns))
    nsb = kb // L_SLC
    lane = lax.broadcasted_iota(jnp.int32, (1, LANE), 1)

    csrc_ref[seq:seq + L_CMP, :] = jnp.zeros((L_CMP, LANE), F32)
    kwin_ref[0:WINDOW, :] = jnp.zeros((WINDOW, dh), BF16)
    vwint_ref[0:nwb] = jnp.zeros((nwb, vr, tk), BF16)
    ones_rows = jnp.where(lax.broadcasted_iota(jnp.int32, (vr - dh, tk), 0) == 0, 1.0, 0.0).astype(BF16)

    def prep(blk):
        r0 = pl.multiple_of(blk * tk, tk)
        cos_t = cos_ref[0, pl.ds(r0, tk), :]
        sin_t = sin_ref[0, pl.ds(r0, tk), :]
        cq = jnp.concatenate([cos_t] * (qw // LANE), axis=1)
        sq = jnp.concatenate([sin_t] * (qw // LANE), axis=1)
        qr = _rope_apply(q_ref[pl.ds(r0, tk), :], cq, sq, perm_ref[...]) * scale
        qt_ref[blk] = qr.T.astype(BF16)
        ck = jnp.where(lane < dh, cos_t, 1.0)
        sk = jnp.where(lane < dh, sin_t, 0.0)
        kv = kv_ref[pl.ds(r0, tk), :]
        perm_kv = perm_ref[0:LANE, 0:LANE]
        csrc_ref[pl.ds(r0, tk), :] = _rope_apply(kv[:, 0:LANE], ck, sk, perm_kv)
        kv_s = _rope_apply(kv[:, LANE:2 * LANE], ck, sk, perm_kv)
        kslc_ref[pl.ds(r0, tk), :] = kv_s[:, :dh].astype(BF16)
        vslct_ref[blk] = jnp.concatenate([kv_s.T[dh:, :].astype(BF16), ones_rows], axis=0)
        kv_w = _rope_apply(kv[:, 2 * LANE:3 * LANE], ck, sk, perm_kv)
        kwin_ref[pl.ds(pl.multiple_of(WINDOW + r0, tk), tk), :] = kv_w[:, :dh].astype(BF16)
        vwint_ref[nwb + blk] = jnp.concatenate([kv_w.T[dh:, :].astype(BF16), ones_rows], axis=0)

    def prep_blocks(i, carry):
        for u in range(NSA_PREP_UNROLL):
            prep(i * NSA_PREP_UNROLL + u)
        return carry

    lax.fori_loop(0, seq // (tk * NSA_PREP_UNROLL), prep_blocks, 0)

    for l in range(L_CMP):
        ccat_ref[:, l * LANE:(l + 1) * LANE] = (
            csrc_ref[pl.ds(l, nc, stride=STRIDE_CMP), :] + pe_ref[l:l + 1, :])
    kvc = _dot(ccat_ref[...], wc_ref[...])
    kc_ref[...] = kvc[:, :dh].astype(BF16)
    vct_ref[...] = kvc.T[dh:, :].astype(BF16)

    n_sub = lax.broadcasted_iota(jnp.int32, (nc, 1), 0)
    j_sub = lax.broadcasted_iota(jnp.int32, (ns, 1), 0)
    k_sub = lax.broadcasted_iota(jnp.int32, (kb, 1), 0)
    c_sub = lax.broadcasted_iota(jnp.int32, (band, 1), 0)
    q_lane = lax.broadcasted_iota(jnp.int32, (1, tq), 1)
    win_static = (c_sub - WINDOW <= q_lane) & (c_sub > q_lane)
    heads = [slice(h * tq, (h + 1) * tq) for h in range(hpg)]

    def q_tile(i, carry):
        r0 = pl.multiple_of(i * tq, tq)
        b0 = i * (tq // tk)
        qblk = [qt_ref[b0 + jb] for jb in range(tq // tk)]
        qt4 = jnp.concatenate([qb[h * dh:(h + 1) * dh, :] for h in range(hpg) for qb in qblk],
                              axis=1)
        t_lane = r0 + q_lane

        s = _dot(kc_ref[...], qt4)
        cmask = (n_sub * STRIDE_CMP + (L_CMP - 1)) <= t_lane
        ps = [_masked_softmax0(s[:, sl], cmask) for sl in heads]
        o_cmp = _dot(vct_ref[...], jnp.concatenate([p.astype(BF16) for p in ps], axis=1))
        psum = ps[0]
        for p in ps[1:]:
            psum = psum + p

        imp = _dot(ovl_ref[...], psum, hi=True)
        cur = t_lane // L_SLC
        causal_blk = j_sub <= cur
        forced = (j_sub == 0) | (j_sub == cur) | (j_sub == cur - 1)
        impv = jnp.where(forced, 1e9, jnp.where(causal_blk, imp, -1e9))
        rank = jnp.zeros((ns, tq), F32)
        for b in range(ns):
            row = impv[b:b + 1, :]
            beats = (row > impv) | ((row == impv) & (b < j_sub))
            rank = rank + jnp.where(beats, 1.0, 0.0)
        sel = jnp.where((rank < n_top) & causal_blk, 1.0, 0.0)
        sel_ref[...] = sel

        s = _dot(kwin_ref[pl.ds(r0, band), :], qt4)
        wmask = win_static & (c_sub >= WINDOW - r0)
        wmb = jnp.where(wmask, 1.0, 0.0).astype(BF16)
        ps = []
        for sl in heads:
            sh = jnp.where(wmask, s[:, sl], -1e30)
            ps.append(_exp_masked_bf16(sh, jnp.max(sh, axis=0, keepdims=True), wmb))
        p4 = jnp.concatenate(ps, axis=1)
        ow = _dot(vwint_ref[b0], p4[0:tk, :])
        for jb in range(1, band // tk):
            ow = ow + _dot(vwint_ref[b0 + jb], p4[jb * tk:(jb + 1) * tk, :])
        o_win = ow[0:dh] * (1.0 / jnp.maximum(ow[dh:dh + 1], 1e-30))

        gt = _dot(gsel_ref[0], jax.nn.sigmoid(sm_ref[pl.ds(r0, tq), :]), NT, hi=True)
        gate = lambda br: jnp.concatenate(
            [jnp.broadcast_to(gt[br * hpg + h:br * hpg + h + 1], (dh, tq)) for h in range(hpg)], axis=1)
        o_cw = gate(0) * o_cmp + gate(2) * o_win
        g_slc = gate(1)

        def kv_step(j, st):
            m, acc = st
            k0 = pl.multiple_of(j * kb, kb)
            s = _dot(kslc_ref[pl.ds(k0, kb), :], qt4)
            sel_rows = sel_ref[pl.ds(pl.multiple_of(j * nsb, nsb), nsb), :]
            selm = jnp.concatenate([jnp.broadcast_to(sel_rows[b:b + 1, :], (L_SLC, tq)) for b in range(nsb)], axis=0)
            msk = (selm > 0.5) & (k0 + k_sub <= t_lane)
            mb = jnp.where(msk, 1.0, 0.0).astype(BF16)
            ms, als, ps = [], [], []
            for sl in heads:
                sh = jnp.where(msk, s[:, sl], -1e30)
                mn = jnp.maximum(m[:, sl], jnp.max(sh, axis=0, keepdims=True))
                ms.append(mn)
                als.append(jnp.exp2(m[:, sl] - mn))
                ps.append(_exp_masked_bf16(sh, mn, mb))
            p4 = jnp.concatenate(ps, axis=1)
            pv = _dot(vslct_ref[j * (kb // tk)], p4[0:tk, :])
            for hf in range(1, kb // tk):
                pv = pv + _dot(vslct_ref[j * (kb // tk) + hf], p4[hf * tk:(hf + 1) * tk, :])
            return jnp.concatenate(ms, axis=1), jnp.concatenate(als, axis=1) * acc + pv

        init = (jnp.full((1, hpg * tq), -1e30, F32), jnp.zeros((vr, hpg * tq), F32))
        _, acc_s = lax.fori_loop(0, (r0 + tq + kb - 1) // kb, kv_step, init)
        o_slc = acc_s[0:dh] * (1.0 / jnp.maximum(acc_s[dh:dh + 1], 1e-30))

        out_t = o_cw + g_slc * o_slc
        out_t = jnp.concatenate([out_t[:, sl] for sl in heads], axis=0)
        o_ref[pl.ds(r0, tq), :] = out_t.T.astype(o_ref.dtype)
        return carry

    lax.fori_loop(0, seq // tq, q_tile, 0)


def _nsa_constants(seq):
    nc = seq // STRIDE_CMP
    ns = seq // L_SLC
    cs = np.arange(nc) * STRIDE_CMP
    js = np.arange(ns) * L_SLC
    ovl_t = ((cs[None, :] < js[:, None] + L_SLC) & (cs[None, :] + L_CMP > js[:, None])).astype(np.float32)
    gsel = np.zeros((NSA_G, 16, LANE), np.float32)
    for g in range(NSA_G):
        for h in range(NSA_HPG):
            for j in range(3):
                gsel[g, j * NSA_HPG + h, SM_GATE + (g * NSA_HPG + h) * 3 + j] = 1.0
    qw = NSA_HPG * NSA_DH
    half = ROPE_DIM // 2
    perm = np.zeros((qw, qw), np.float32)
    for l in range(qw):
        d = l % NSA_DH
        if d < ROPE_DIM:
            perm[l + half if d < half else l - half, l] = 1.0
    return jnp.asarray(ovl_t), jnp.asarray(gsel), jnp.asarray(perm, dtype=BF16)


def _nsa(proj, q_blk, kv_blk, sm_blk, bsz, seq, cos_t, sin_t, pe_k, pe_v, wk_cmp, wv_cmp, consts):
    t = bsz * seq
    dh = NSA_DH
    qw = NSA_HPG * dh
    tq, tk = NSA_TQ, NSA_TK
    nc = seq // STRIDE_CMP
    ns = seq // L_SLC
    ovl_t, gsel, perm = consts
    pe = jnp.concatenate([pe_k, pe_v], axis=1)
    zero = jnp.zeros((L_CMP, dh, dh), F32)
    wc = jnp.concatenate([jnp.concatenate([wk_cmp, zero], axis=2),
                          jnp.concatenate([zero, wv_cmp], axis=2)], axis=1)
    wc = wc.reshape(L_CMP * LANE, LANE).astype(BF16)
    kern = functools.partial(_nsa_kernel, seq=seq)
    full = lambda shape: pl.BlockSpec(shape, lambda b, g: tuple(0 for _ in shape))
    return pl.pallas_call(
        kern,
        grid=(bsz, NSA_G),
        in_specs=[pl.BlockSpec((seq, qw), lambda b, g: (b, q_blk + g)),
                  pl.BlockSpec((seq, 3 * LANE), lambda b, g: (b, kv_blk + g)),
                  pl.BlockSpec((seq, LANE), lambda b, g: (b, sm_blk)),
                  pl.BlockSpec((1, seq, LANE), lambda b, g: (b, 0, 0)),
                  pl.BlockSpec((1, seq, LANE), lambda b, g: (b, 0, 0)),
                  full((L_CMP, LANE)),
                  full((L_CMP * LANE, LANE)),
                  full((ns, nc)),
                  pl.BlockSpec((1, 16, LANE), lambda b, g: (g, 0, 0)),
                  full((qw, qw))],
        out_specs=pl.BlockSpec((seq, qw), lambda b, g: (b, g)),
        out_shape=jax.ShapeDtypeStruct((t, NSA_G * qw), BF16),
        scratch_shapes=[pltpu.VMEM((seq // tk, qw, tk), BF16),
                        pltpu.VMEM((seq + L_CMP, LANE), F32),
                        pltpu.VMEM((nc, L_CMP * LANE), F32),
                        pltpu.VMEM((nc, dh), BF16),
                        pltpu.VMEM((dh, nc), BF16),
                        pltpu.VMEM((seq, dh), BF16),
                        pltpu.VMEM((seq // tk, NSA_VR, tk), BF16),
                        pltpu.VMEM((WINDOW + seq, dh), BF16),
                        pltpu.VMEM(((WINDOW + seq) // tk, NSA_VR, tk), BF16),
                        pltpu.VMEM((ns, tq), F32)],
        compiler_params=_params("parallel", "parallel"),
        name="nsa",
    )(proj, proj, proj, cos_t, sin_t, pe, wc, ovl_t, gsel, perm)


def _dot3(a, b):
    a_hi = a.astype(BF16)
    a_lo = (a - a_hi.astype(F32)).astype(BF16)
    b_hi = b.astype(BF16)
    b_lo = (b - b_hi.astype(F32)).astype(BF16)
    d = lambda x, y: jnp.dot(x, y, preferred_element_type=F32)
    return d(a_hi, b_hi) + (d(a_lo, b_hi) + d(a_hi, b_lo))


def _unit_lower_solve(lmats, rhss):
    c, width = rhss[0].shape
    bs = GDN_SOLVE_BLOCK
    done = [[] for _ in rhss]
    for b in range(c // bs):
        r0 = b * bs
        xbs = []
        for lmat, rhs, dn in zip(lmats, rhss, done):
            xb = rhs[r0:r0 + bs]
            if b:
                partial = jnp.concatenate(dn + [jnp.zeros((c - r0, width), F32)], axis=0)
                xb = xb - _dot3(lmat[r0:r0 + bs, :], partial)
            xbs.append(xb)
        for lmat, xb, dn in zip(lmats, xbs, done):
            lb = lmat[r0:r0 + bs, r0:r0 + bs]
            groups = [xb[8 * g:8 * g + 8] for g in range(bs // 8)]
            for j in range(bs - 1):
                row = groups[j // 8][j % 8:j % 8 + 1, :]
                for g in range(j // 8, bs // 8):
                    if 8 * g + 7 > j:
                        groups[g] = groups[g] - lb[8 * g:8 * g + 8, j:j + 1] * row
            dn.append(jnp.concatenate(groups, axis=0))
    return [jnp.concatenate(dn, axis=0) for dn in done]


def _gdn_kernel(qkv_ref, z_ref, sm_ref, cw_ref, alog_ref, dtb_ref, ng_ref, o_ref, buf_ref, xc_ref, st_ref,
                *, k_len):
    c = GDN_CHUNK
    rows = c * GDN_STEP_CHUNKS
    dk, dv = GDN_DK, GDN_DV

    @pl.when(pl.program_id(1) == 0)
    def _():
        buf_ref[0:GDN_HALO, :] = jnp.zeros((GDN_HALO, GDN_QKV), F32)
        st_ref[...] = jnp.zeros(st_ref.shape, F32)

    x = qkv_ref[...]
    buf_ref[GDN_HALO:GDN_HALO + rows, :] = x
    base = GDN_HALO - (k_len - 1)
    acc = jnp.zeros((rows, GDN_QKV), F32)
    for k in range(k_len):
        acc = acc + cw_ref[k:k + 1, :] * buf_ref[base + k:base + k + rows, :]
    buf_ref[0:GDN_HALO, :] = x[rows - GDN_HALO:rows, :]
    xc_ref[...] = acc * jax.nn.sigmoid(acc)

    sm = sm_ref[...]
    beta_all = jax.nn.sigmoid(sm)
    g_all = -jnp.exp(alog_ref[...]) * jax.nn.softplus(sm + dtb_ref[...])
    ri = lax.broadcasted_iota(jnp.int32, (rows, rows), 0)
    rj = lax.broadcasted_iota(jnp.int32, (rows, rows), 1)
    same_chunk_incl = (ri >= rj) & (ri // c == rj // c)
    gc_all = _dot(jnp.where(same_chunk_incl, 1.0, 0.0), g_all, hi=True)
    gc_rows = gc_all.T
    ii = lax.broadcasted_iota(jnp.int32, (c, c), 0)
    jj = lax.broadcasted_iota(jnp.int32, (c, c), 1)
    incl = ii >= jj
    strict = ii > jj

    pre, lmats, rhss = [], [], []
    for n in range(GDN_STEP_CHUNKS):
        rs = slice(n * c, (n + 1) * c)
        for h in range(GDN_H):
            gcol = gc_all[rs, SM_A + h:SM_A + h + 1]
            bcol = beta_all[rs, SM_BETA + h:SM_BETA + h + 1]
            grow = gc_rows[SM_A + h:SM_A + h + 1, rs]
            decay = jnp.exp(jnp.where(incl, gcol - grow, -jnp.inf))
            q = xc_ref[rs, h * dk:(h + 1) * dk]
            k = xc_ref[rs, GDN_H * dk + h * dk:GDN_H * dk + (h + 1) * dk]
            v = xc_ref[rs, 2 * GDN_H * dk + h * dv:2 * GDN_H * dk + (h + 1) * dv]
            q = q * lax.rsqrt(jnp.sum(q * q, axis=-1, keepdims=True) + EPS) * (dk ** -0.5)
            k = k * lax.rsqrt(jnp.sum(k * k, axis=-1, keepdims=True) + EPS)
            eg = jnp.exp(gcol)
            lmat = jnp.where(strict, bcol * _dot(k, k, NT) * decay, 0.0)
            qk = jnp.where(incl, _dot(q, k, NT) * decay, 0.0)
            glast = gcol[c - 1:c, :]
            lmats.append(lmat)
            rhss.append(jnp.concatenate([k * (bcol * eg), v * bcol], axis=1))
            pre.append((q * eg, k * jnp.exp(glast - gcol), qk, jnp.exp(glast)))
    sols = _unit_lower_solve(lmats, rhss)

    z = z_ref[...]
    for n in range(GDN_STEP_CHUNKS):
        rs = slice(n * c, (n + 1) * c)
        idx = range(n * GDN_H, (n + 1) * GDN_H)
        states = [st_ref[h] for h in range(GDN_H)]
        v_news = [sols[i][:, dk:] - _dot(sols[i][:, :dk], st) for i, st in zip(idx, states)]
        o_state = [_dot(pre[i][0], st) for i, st in zip(idx, states)]
        o_intra = [_dot(pre[i][2], vn) for i, vn in zip(idx, v_news)]
        s_add = [_dot(pre[i][1], vn, TN) for i, vn in zip(idx, v_news)]
        for h, i in enumerate(idx):
            st_ref[h] = pre[i][3] * states[h] + s_add[h]
            o = o_state[h] + o_intra[h]
            o = o * lax.rsqrt(jnp.mean(o * o, axis=-1, keepdims=True) + EPS) * ng_ref[...]
            zh = z[rs, h * dv:(h + 1) * dv]
            o_ref[rs, h * dv:(h + 1) * dv] = (o * (zh * jax.nn.sigmoid(zh))).astype(o_ref.dtype)


def _gdn(proj, qkv_blk, z_blk, sm_blk, bsz, seq, conv_w, a_log, dt_bias, norm_g):
    t = bsz * seq
    c = GDN_CHUNK * GDN_STEP_CHUNKS
    nch = seq // c
    k_len = conv_w.shape[0]
    alog = jnp.zeros((1, LANE), F32).at[0, SM_A:SM_A + GDN_H].set(a_log)
    dtb = jnp.zeros((1, LANE), F32).at[0, SM_A:SM_A + GDN_H].set(dt_bias)
    kern = functools.partial(_gdn_kernel, k_len=k_len)
    full = lambda shape: pl.BlockSpec(shape, lambda b, n: tuple(0 for _ in shape))
    return pl.pallas_call(
        kern,
        grid=(bsz, nch),
        in_specs=[pl.BlockSpec((c, GDN_QKV), lambda b, n: (b * nch + n, qkv_blk)),
                  pl.BlockSpec((c, GDN_WIDTH), lambda b, n: (b * nch + n, z_blk)),
                  pl.BlockSpec((c, LANE), lambda b, n: (b * nch + n, sm_blk)),
                  full((k_len, GDN_QKV)), full((1, LANE)), full((1, LANE)), full((1, GDN_DV))],
        out_specs=pl.BlockSpec((c, GDN_WIDTH), lambda b, n: (b * nch + n, 0)),
        out_shape=jax.ShapeDtypeStruct((t, GDN_WIDTH), BF16),
        scratch_shapes=[pltpu.VMEM((GDN_HALO + c, GDN_QKV), F32),
                        pltpu.VMEM((c, GDN_QKV), F32),
                        pltpu.VMEM((GDN_H, GDN_DK, GDN_DV), F32)],
        compiler_params=_params("parallel", "arbitrary"),
        name="gdn",
    )(proj, proj, proj, conv_w, alog, dtb, norm_g.reshape(1, -1))


def _merge_kernel(x_ref, mod_ref, yc_ref, yn_ref, yg_ref, gm_ref, wc_ref, wn_ref, wg_ref, wo_ref, o_ref):
    d = x_ref.shape[1]
    gm = jax.nn.sigmoid(gm_ref[...])
    merged = (gm[:, 0:d] * jnp.dot(yc_ref[...], wc_ref[...], preferred_element_type=F32)
              + gm[:, d:2 * d] * jnp.dot(yn_ref[...], wn_ref[...], preferred_element_type=F32)
              + gm[:, 2 * d:3 * d] * jnp.dot(yg_ref[...], wg_ref[...], preferred_element_type=F32))
    gt = mod_ref[0][2:3]
    o_ref[...] = x_ref[...] + gt * _dot(merged, wo_ref[...])


def _merge(x2, mod_l, y_conv, y_nsa, y_gdn, proj, gm_blk, w_c, w_n, w_g, w_o, seq, tm=512):
    t, d = x2.shape
    per_b = seq // tm
    row = lambda w: pl.BlockSpec((tm, w), lambda i: (i, 0))
    wspec = lambda w: pl.BlockSpec(w.shape, lambda i: (0, 0))
    return pl.pallas_call(
        _merge_kernel,
        grid=(t // tm,),
        in_specs=[row(d),
                  pl.BlockSpec((1, N_MOD, d), lambda i: (i // per_b, 0, 0)),
                  row(y_conv.shape[1]), row(y_nsa.shape[1]), row(y_gdn.shape[1]),
                  pl.BlockSpec((tm, 3 * d), lambda i: (i, gm_blk)),
                  wspec(w_c), wspec(w_n), wspec(w_g), wspec(w_o)],
        out_specs=row(d),
        out_shape=jax.ShapeDtypeStruct((t, d), F32),
        compiler_params=_params("parallel"),
        name="merge",
    )(x2, mod_l, y_conv, y_nsa, y_gdn, proj, w_c, w_n, w_g, w_o)


def _ffn_kernel(x_ref, mod_ref, g_ref, wg_ref, wu_ref, wo_ref, fg_ref, o_ref, h_ref, acc_ref, *, final):
    j = pl.program_id(1)

    @pl.when(j == 0)
    def _():
        m = mod_ref[0]
        h_ref[...] = _norm_mod(x_ref[...], g_ref[...], m[3:4], m[4:5]).astype(BF16)
        acc_ref[...] = jnp.zeros(acc_ref.shape, F32)

    h = h_ref[...]
    gate = jnp.dot(h, wg_ref[...], preferred_element_type=F32)
    up = jnp.dot(h, wu_ref[...], preferred_element_type=F32)
    act = (gate * jax.nn.sigmoid(gate)) * up
    acc_ref[...] += _dot(act, wo_ref[...])

    @pl.when(j == pl.num_programs(1) - 1)
    def _():
        y = x_ref[...] + mod_ref[0][5:6] * acc_ref[...]
        if final:
            y = y * lax.rsqrt(jnp.mean(y * y, axis=-1, keepdims=True) + EPS) * fg_ref[...]
        o_ref[...] = y


def _ffn(x2, mod_l, g, w_in, w_out, final_g, seq, final, tm=512, th=1408):
    t, d = x2.shape
    hid = w_out.shape[0]
    per_b = seq // tm
    nh = hid // th
    kern = functools.partial(_ffn_kernel, final=final)
    return pl.pallas_call(
        kern,
        grid=(t // tm, nh),
        in_specs=[pl.BlockSpec((tm, d), lambda i, j: (i, 0)),
                  pl.BlockSpec((1, N_MOD, d), lambda i, j: (i // per_b, 0, 0)),
                  pl.BlockSpec((1, d), lambda i, j: (0, 0)),
                  pl.BlockSpec((d, th), lambda i, j: (0, j)),
                  pl.BlockSpec((d, th), lambda i, j: (0, nh + j)),
                  pl.BlockSpec((th, d), lambda i, j: (j, 0)),
                  pl.BlockSpec((1, d), lambda i, j: (0, 0))],
        out_specs=pl.BlockSpec((tm, d), lambda i, j: (i, 0)),
        out_shape=jax.ShapeDtypeStruct((t, d), F32),
        scratch_shapes=[pltpu.VMEM((tm, d), BF16), pltpu.VMEM((tm, d), F32)],
        compiler_params=_params("parallel", "arbitrary"),
        name="ffn",
    )(x2, mod_l, g, w_in, w_in, w_out, final_g)


def _in_layout(d):
    segs = {}
    off = 0
    for name, width in (("gm", 3 * d), ("conv", 2 * C_CONV), ("q", NSA_WIDTH), ("qkv", GDN_QKV),
                        ("z", GDN_WIDTH), ("small", LANE), ("pad", LANE),
                        ("kv", 6 * NSA_G * NSA_DH)):
        segs[name] = off
        off += width
    return segs, off


def _rearrange_in_proj(w_in, b_in, d):
    splits = (2 * C_CONV, NSA_WIDTH, 6 * NSA_G * NSA_DH, 3 * NSA_HEADS, GDN_QKV, GDN_WIDTH, GDN_H, GDN_H, 3 * d)
    offs = np.concatenate([[0], np.cumsum(splits)])
    names = ("conv", "q", "kv", "gnsa", "qkv", "z", "beta", "a", "gm")
    segs, total = _in_layout(d)

    def arrange(a):
        src = {n: a[..., int(offs[i]):int(offs[i + 1])] for i, n in enumerate(names)}
        lead = a.shape[:-1]
        kv = jnp.swapaxes(src["kv"].reshape(lead + (6, NSA_G, NSA_DH)), -3, -2).reshape(lead + (-1,))
        small_pad = jnp.zeros(lead + (LANE - 3 * NSA_HEADS - 2 * GDN_H,), a.dtype)
        pad = jnp.zeros(lead + (LANE,), a.dtype)
        out = jnp.concatenate([src["gm"], src["conv"], src["q"], src["qkv"], src["z"],
                               src["gnsa"], src["beta"], src["a"], small_pad, pad, kv], axis=-1)
        assert out.shape[-1] == total
        return out

    return arrange(w_in.astype(BF16)), arrange(b_in)[:, None, :], segs


def kernel(x, c, positions, ada_w, ada_b, norm_mix_g, norm_ffn_g, w_in, b_in, conv_dw_w, conv_dw_b, conv_ln_g, conv_ln_b, nsa_pe_k, nsa_pe_v, nsa_wk_cmp, nsa_wv_cmp, gdn_conv_w, gdn_a_log, gdn_dt_bias, gdn_norm_g, w_up_conv, w_up_nsa, w_up_gdn, w_o, ffn_w_in, ffn_w_out, final_norm_g):
    bsz, seq, d = x.shape
    depth = ada_w.shape[0]
    t = bsz * seq

    mod = _modulation(c, ada_w, ada_b).reshape(depth, bsz, N_MOD, d)
    cos_t, sin_t = _rope_tables(positions)
    nsa_consts = _nsa_constants(seq)
    w_in_r, b_in_r, segs = _rearrange_in_proj(w_in, b_in, d)
    blk = lambda name, width: segs[name] // width
    qw = NSA_HPG * NSA_DH

    x2 = x.reshape(t, d)
    for l in range(depth):
        proj = _in_projection(x2, mod[l], norm_mix_g[l].reshape(1, d), w_in_r, b_in_r, l, seq)
        y_conv = _conformer_conv(proj, blk("conv", 2 * C_CONV), bsz, seq,
                                 conv_dw_w[l], conv_dw_b[l], conv_ln_g[l], conv_ln_b[l])
        y_nsa = _nsa(proj, blk("q", qw), blk("kv", 3 * LANE), blk("small", LANE), bsz, seq, cos_t, sin_t,
                     nsa_pe_k[l], nsa_pe_v[l], nsa_wk_cmp[l], nsa_wv_cmp[l], nsa_consts)
        y_gdn = _gdn(proj, blk("qkv", GDN_QKV), blk("z", GDN_WIDTH), blk("small", LANE), bsz, seq,
                     gdn_conv_w[l], gdn_a_log[l], gdn_dt_bias[l], gdn_norm_g[l])
        x2 = _merge(x2, mod[l], y_conv, y_nsa, y_gdn, proj, blk("gm", 3 * d),
                    w_up_conv[l].astype(BF16), w_up_nsa[l].astype(BF16), w_up_gdn[l].astype(BF16),
                    w_o[l].astype(BF16), seq)
        x2 = _ffn(x2, mod[l], norm_ffn_g[l].reshape(1, d), ffn_w_in[l].astype(BF16),
                  ffn_w_out[l].astype(BF16), final_norm_g.reshape(1, d), seq, final=(l == depth - 1))
    return x2.reshape(bsz, seq, d)
```

```python
import functools

import numpy as np
import jax
import jax.numpy as jnp
from jax import lax
from jax.experimental import pallas as pl
from jax.experimental.pallas import tpu as pltpu

F32 = jnp.float32
BF16 = jnp.bfloat16
HI = lax.Precision.HIGHEST
NT = (((1,), (1,)), ((), ()))
TN = (((0,), (0,)), ((), ()))

EPS = 1e-6
LOG2_E = 1.4426950408889634
N_MOD = 6
C_CONV = 512
CONV_HALO = 32
NSA_HEADS = 8
NSA_G = 2
NSA_HPG = NSA_HEADS // NSA_G
NSA_DH = 64
NSA_WIDTH = NSA_HEADS * NSA_DH
ROPE_DIM = NSA_DH // 4
ROPE_THETA = 500000.0
L_CMP = 32
STRIDE_CMP = 16
L_SLC = 64
N_SLC_TOP = 8
WINDOW = 512
NSA_TQ = 512
NSA_TK = 128
NSA_PREP_UNROLL = 4
NSA_KB = 512
NSA_VR = NSA_DH + 16
GDN_H = 4
GDN_DK = 128
GDN_DV = 128
GDN_WIDTH = GDN_H * GDN_DV
GDN_QKV = GDN_H * (2 * GDN_DK + GDN_DV)
GDN_CHUNK = 64
GDN_HALO = 8
GDN_STEP_CHUNKS = 4
GDN_SOLVE_BLOCK = 8
SM_GATE = 0
SM_BETA = 24
SM_A = 28
LANE = 128

VMEM_LIMIT = 56 * 1024 * 1024


def _dot(a, b, dims=None, hi=False):
    if dims is None:
        dims = (((a.ndim - 1,), (0,)), ((), ()))
    if hi:
        return lax.dot_general(a.astype(F32), b.astype(F32), dims, precision=HI,
                               preferred_element_type=F32)
    return lax.dot_general(a.astype(BF16), b.astype(BF16), dims, preferred_element_type=F32)


def _masked_softmax0(s, mask):
    s = jnp.where(mask, s, -1e30)
    m = jnp.max(s, axis=0, keepdims=True)
    p = jnp.where(mask, jnp.exp2(s - m), 0.0)
    return p * (1.0 / jnp.maximum(jnp.sum(p, axis=0, keepdims=True), 1e-30))


def _exp_masked_bf16(sh, m, mb):
    return jnp.exp2((sh - m).astype(BF16)) * mb


def _params(*sem):
    return pltpu.CompilerParams(dimension_semantics=sem, vmem_limit_bytes=VMEM_LIMIT)


def _mod_kernel(c_ref, w_ref, b_ref, o_ref):
    c = c_ref[...]
    ca = c * jax.nn.sigmoid(c)
    o_ref[0] = _dot(ca, w_ref[0], hi=True) + b_ref[0]


def _modulation(c, ada_w, ada_b):
    depth, d, n = ada_w.shape
    bsz = c.shape[0]
    tn = 1536
    return pl.pallas_call(
        _mod_kernel,
        grid=(depth, n // tn),
        in_specs=[pl.BlockSpec((bsz, d), lambda l, j: (0, 0)),
                  pl.BlockSpec((1, d, tn), lambda l, j: (l, 0, j)),
                  pl.BlockSpec((1, 1, tn), lambda l, j: (l, 0, j))],
        out_specs=pl.BlockSpec((1, bsz, tn), lambda l, j: (l, 0, j)),
        out_shape=jax.ShapeDtypeStruct((depth, bsz, n), F32),
        compiler_params=_params("parallel", "parallel"),
        name="adaln_mod",
    )(c, ada_w, ada_b.reshape(depth, 1, n))


def _norm_mod(x, g, sh, sc):
    y = x * lax.rsqrt(jnp.mean(x * x, axis=-1, keepdims=True) + EPS)
    return (y * g) * (1.0 + sc) + sh


def _inproj_kernel(x_ref, mod_ref, g_ref, w_ref, b_ref, o_ref, h_ref):
    @pl.when(pl.program_id(1) == 0)
    def _():
        m = mod_ref[0]
        h_ref[...] = _norm_mod(x_ref[...], g_ref[...], m[0:1], m[1:2]).astype(BF16)

    o_ref[...] = jnp.dot(h_ref[...], w_ref[0], preferred_element_type=F32) + b_ref[0]


def _in_projection(x2, mod_l, g, w, b, layer, seq, tm=1024, tn=1536):
    t, d = x2.shape
    n = w.shape[2]
    per_b = seq // tm
    return pl.pallas_call(
        _inproj_kernel,
        grid=(t // tm, n // tn),
        in_specs=[pl.BlockSpec((tm, d), lambda i, j: (i, 0)),
                  pl.BlockSpec((1, N_MOD, d), lambda i, j: (i // per_b, 0, 0)),
                  pl.BlockSpec((1, d), lambda i, j: (0, 0)),
                  pl.BlockSpec((1, d, tn), lambda i, j: (layer, 0, j)),
                  pl.BlockSpec((1, 1, tn), lambda i, j: (layer, 0, j))],
        out_specs=pl.BlockSpec((tm, tn), lambda i, j: (i, j)),
        out_shape=jax.ShapeDtypeStruct((t, n), F32),
        scratch_shapes=[pltpu.VMEM((tm, d), BF16)],
        compiler_params=_params("parallel", "arbitrary"),
        name="in_proj",
    )(x2, mod_l, g, w, b)


def _conv_kernel(u_ref, w_ref, b_ref, lg_ref, lb_ref, o_ref, buf_ref, *, ts, k_len, rc):
    c = C_CONV

    @pl.when(pl.program_id(1) == 0)
    def _():
        buf_ref[0:CONV_HALO, :] = jnp.zeros((CONV_HALO, c), F32)

    @pl.when(pl.program_id(1) > 0)
    def _():
        buf_ref[0:CONV_HALO, :] = buf_ref[ts:ts + CONV_HALO, :]

    u = u_ref[...]
    buf_ref[CONV_HALO:CONV_HALO + ts, :] = u[:, :c] * jax.nn.sigmoid(u[:, c:])
    base = CONV_HALO - (k_len - 1)

    def chunk(ci, carry):
        r0 = pl.multiple_of(ci * rc, rc)
        win = buf_ref[pl.ds(r0, rc + CONV_HALO), :]
        acc = jnp.zeros((rc // 8, 8, c), F32)
        for ph in range(8):
            taps = [k for k in range(k_len) if (base + k) % 8 == ph]
            if not taps:
                continue
            span = max(base + k for k in taps) - ph + rc
            shifted = win[ph:ph + span, :]
            for k in taps:
                off = base + k - ph
                acc = acc + w_ref[k][None] * shifted[off:off + rc, :].reshape(rc // 8, 8, c)
        y = acc.reshape(rc, c) + b_ref[...]
        mu = jnp.mean(y, axis=-1, keepdims=True)
        var = jnp.mean(jnp.square(y - mu), axis=-1, keepdims=True)
        y = (y - mu) * lax.rsqrt(var + EPS) * lg_ref[...] + lb_ref[...]
        o_ref[pl.ds(r0, rc), :] = (y * jax.nn.sigmoid(y)).astype(o_ref.dtype)
        return carry

    lax.fori_loop(0, ts // rc, chunk, 0)


def _conformer_conv(proj, col_blk, bsz, seq, dw_w, dw_b, ln_g, ln_b, ts=512, rc=128):
    t = bsz * seq
    k_len = dw_w.shape[0]
    per_b = seq // ts
    kern = functools.partial(_conv_kernel, ts=ts, k_len=k_len, rc=rc)
    vec = lambda: pl.BlockSpec((1, C_CONV), lambda b, s: (0, 0))
    return pl.pallas_call(
        kern,
        grid=(bsz, per_b),
        in_specs=[pl.BlockSpec((ts, 2 * C_CONV), lambda b, s: (b * per_b + s, col_blk)),
                  pl.BlockSpec((k_len, 8, C_CONV), lambda b, s: (0, 0, 0)),
                  vec(), vec(), vec()],
        out_specs=pl.BlockSpec((ts, C_CONV), lambda b, s: (b * per_b + s, 0)),
        out_shape=jax.ShapeDtypeStruct((t, C_CONV), BF16),
        scratch_shapes=[pltpu.VMEM((CONV_HALO + ts, C_CONV), F32)],
        compiler_params=_params("parallel", "arbitrary"),
        name="conformer_conv",
    )(proj, jnp.broadcast_to(dw_w[:, None, :], (k_len, 8, C_CONV)),
      dw_b.reshape(1, -1), ln_g.reshape(1, -1), ln_b.reshape(1, -1))


def _rope_table_kernel(pos_ref, inv_ref, sign_ref, cos_ref, sin_ref):
    ang = pos_ref[0].astype(F32) * inv_ref[...]
    cos_ref[0] = jnp.cos(ang)
    sin_ref[0] = jnp.sin(ang) * sign_ref[...]


def _rope_tables(positions):
    bsz, seq = positions.shape
    half = ROPE_DIM // 2
    inv = ROPE_THETA ** (-jnp.arange(half, dtype=F32) * 2.0 / ROPE_DIM)
    d = np.arange(LANE) % NSA_DH
    inv_lane = jnp.where(jnp.asarray(d < ROPE_DIM), inv[d % half], 0.0).reshape(1, LANE)
    sign_lane = jnp.asarray(np.where(d < half, -1.0, np.where(d < ROPE_DIM, 1.0, 0.0)),
                            dtype=F32).reshape(1, LANE)
    out = jax.ShapeDtypeStruct((bsz, seq, LANE), F32)
    return pl.pallas_call(
        _rope_table_kernel,
        grid=(bsz,),
        in_specs=[pl.BlockSpec((1, seq, 1), lambda b: (b, 0, 0)),
                  pl.BlockSpec((1, LANE), lambda b: (0, 0)),
                  pl.BlockSpec((1, LANE), lambda b: (0, 0))],
        out_specs=[pl.BlockSpec((1, seq, LANE), lambda b: (b, 0, 0))] * 2,
        out_shape=[out, out],
        compiler_params=_params("parallel"),
        name="rope_tables",
    )(positions.reshape(bsz, seq, 1), inv_lane, sign_lane)


def _rope_apply(x, cos_t, sin_t, perm):
    partner = jnp.dot(x.astype(BF16), perm, preferred_element_type=F32)
    return x * cos_t + partner * sin_t


def _nsa_kernel(q_ref, kv_ref, sm_ref, cos_ref, sin_ref, pe_ref, wc_ref, ovl_ref, gsel_ref, perm_ref,
                o_ref,
                qt_ref, csrc_ref, ccat_ref, kc_ref, vct_ref, kslc_ref, vslct_ref, kwin_ref, vwint_ref,
                sel_ref, *, seq):
    tq, tk, kb, dh, hpg, vr = NSA_TQ, NSA_TK, NSA_KB, NSA_DH, NSA_HPG, NSA_VR
    nc = seq // STRIDE_CMP
    ns = seq // L_SLC
    nwb = WINDOW // tk
    band = WINDOW + tq
    scale = dh ** -0.5 * LOG2_E
    qw = hpg * dh
    n_top = float(min(N_SLC_TOP, ns))
    nsb = kb // L_SLC
    lane = lax.broadcasted_iota(jnp.int32, (1, LANE), 1)

    csrc_ref[seq:seq + L_CMP, :] = jnp.zeros((L_CMP, LANE), F32)
    kwin_ref[0:WINDOW, :] = jnp.zeros((WINDOW, dh), BF16)
    vwint_ref[0:nwb] = jnp.zeros((nwb, vr, tk), BF16)
    ones_rows = jnp.where(lax.broadcasted_iota(jnp.int32, (vr - dh, tk), 0) == 0, 1.0, 0.0).astype(BF16)

    def prep(blk):
        r0 = pl.multiple_of(blk * tk, tk)
        cos_t = cos_ref[0, pl.ds(r0, tk), :]
        sin_t = sin_ref[0, pl.ds(r0, tk), :]
        cq = jnp.concatenate([cos_t] * (qw // LANE), axis=1)
        sq = jnp.concatenate([sin_t] * (qw // LANE), axis=1)
        qr = _rope_apply(q_ref[pl.ds(r0, tk), :], cq, sq, perm_ref[...]) * scale
        qt_ref[blk] = qr.T.astype(BF16)
        ck = jnp.where(lane < dh, cos_t, 1.0)
        sk = jnp.where(lane < dh, sin_t, 0.0)
        kv = kv_ref[pl.ds(r0, tk), :]
        perm_kv = perm_ref[0:LANE, 0:LANE]
        csrc_ref[pl.ds(r0, tk), :] = _rope_apply(kv[:, 0:LANE], ck, sk, perm_kv)
        kv_s = _rope_apply(kv[:, LANE:2 * LANE], ck, sk, perm_kv)
        kslc_ref[pl.ds(r0, tk), :] = kv_s[:, :dh].astype(BF16)
        vslct_ref[blk] = jnp.concatenate([kv_s.T[dh:, :].astype(BF16), ones_rows], axis=0)
        kv_w = _rope_apply(kv[:, 2 * LANE:3 * LANE], ck, sk, perm_kv)
        kwin_ref[pl.ds(pl.multiple_of(WINDOW + r0, tk), tk), :] = kv_w[:, :dh].astype(BF16)
        vwint_ref[nwb + blk] = jnp.concatenate([kv_w.T[dh:, :].astype(BF16), ones_rows], axis=0)

    def prep_blocks(i, carry):
        for u in range(NSA_PREP_UNROLL):
            prep(i * NSA_PREP_UNROLL + u)
        return carry

    lax.fori_loop(0, seq // (tk * NSA_PREP_UNROLL), prep_blocks, 0)

    for l in range(L_CMP):
        ccat_ref[:, l * LANE:(l + 1) * LANE] = (
            csrc_ref[pl.ds(l, nc, stride=STRIDE_CMP), :] + pe_ref[l:l + 1, :])
    kvc = _dot(ccat_ref[...], wc_ref[...])
    kc_ref[...] = kvc[:, :dh].astype(BF16)
    vct_ref[...] = kvc.T[dh:, :].astype(BF16)

    n_sub = lax.broadcasted_iota(jnp.int32, (nc, 1), 0)
    j_sub = lax.broadcasted_iota(jnp.int32, (ns, 1), 0)
    k_sub = lax.broadcasted_iota(jnp.int32, (kb, 1), 0)
    c_sub = lax.broadcasted_iota(jnp.int32, (band, 1), 0)
    q_lane = lax.broadcasted_iota(jnp.int32, (1, tq), 1)
    win_static = (c_sub - WINDOW <= q_lane) & (c_sub > q_lane)
    heads = [slice(h * tq, (h + 1) * tq) for h in range(hpg)]

    def q_tile(i, carry):
        r0 = pl.multiple_of(i * tq, tq)
        b0 = i * (tq // tk)
        qblk = [qt_ref[b0 + jb] for jb in range(tq // tk)]
        qt4 = jnp.concatenate([qb[h * dh:(h + 1) * dh, :] for h in range(hpg) for qb in qblk],
                              axis=1)
        t_lane = r0 + q_lane

        s = _dot(kc_ref[...], qt4)
        cmask = (n_sub * STRIDE_CMP + (L_CMP - 1)) <= t_lane
        ps = [_masked_softmax0(s[:, sl], cmask) for sl in heads]
        o_cmp = _dot(vct_ref[...], jnp.concatenate([p.astype(BF16) for p in ps], axis=1))
        psum = ps[0]
        for p in ps[1:]:
            psum = psum + p

        imp = _dot(ovl_ref[...], psum, hi=True)
        cur = t_lane // L_SLC
        causal_blk = j_sub <= cur
        forced = (j_sub == 0) | (j_sub == cur) | (j_sub == cur - 1)
        impv = jnp.where(forced, 1e9, jnp.where(causal_blk, imp, -1e9))
        rank = jnp.zeros((ns, tq), F32)
        for b in range(ns):
            row = impv[b:b + 1, :]
            beats = (row > impv) | ((row == impv) & (b < j_sub))
            rank = rank + jnp.where(beats, 1.0, 0.0)
        sel = jnp.where((rank < n_top) & causal_blk, 1.0, 0.0)
        sel_ref[...] = sel

        s = _dot(kwin_ref[pl.ds(r0, band), :], qt4)
        wmask = win_static & (c_sub >= WINDOW - r0)
        wmb = jnp.where(wmask, 1.0, 0.0).astype(BF16)
        ps = []
        for sl in heads:
            sh = jnp.where(wmask, s[:, sl], -1e30)
            ps.append(_exp_masked_bf16(sh, jnp.max(sh, axis=0, keepdims=True), wmb))
        p4 = jnp.concatenate(ps, axis=1)
        ow = _dot(vwint_ref[b0], p4[0:tk, :])
        for jb in range(1, band // tk):
            ow = ow + _dot(vwint_ref[b0 + jb], p4[jb * tk:(jb + 1) * tk, :])
        o_win = ow[0:dh] * (1.0 / jnp.maximum(ow[dh:dh + 1], 1e-30))

        gt = _dot(gsel_ref[0], jax.nn.sigmoid(sm_ref[pl.ds(r0, tq), :]), NT, hi=True)
        gate = lambda br: jnp.concatenate(
            [jnp.broadcast_to(gt[br * hpg + h:br * hpg + h + 1], (dh, tq)) for h in range(hpg)], axis=1)
        o_cw = gate(0) * o_cmp + gate(2) * o_win
        g_slc = gate(1)

        def kv_step(j, st):
            m, acc = st
            k0 = pl.multiple_of(j * kb, kb)
            s = _dot(kslc_ref[pl.ds(k0, kb), :], qt4)
            sel_rows = sel_ref[pl.ds(pl.multiple_of(j * nsb, nsb), nsb), :]
            selm = jnp.concatenate([jnp.broadcast_to(sel_rows[b:b + 1, :], (L_SLC, tq)) for b in range(nsb)], axis=0)
            msk = (selm > 0.5) & (k0 + k_sub <= t_lane)
            mb = jnp.where(msk, 1.0, 0.0).astype(BF16)
            ms, als, ps = [], [], []
            for sl in heads:
                sh = jnp.where(msk, s[:, sl], -1e30)
                mn = jnp.maximum(m[:, sl], jnp.max(sh, axis=0, keepdims=True))
                ms.append(mn)
                als.append(jnp.exp2(m[:, sl] - mn))
                ps.append(_exp_masked_bf16(sh, mn, mb))
            p4 = jnp.concatenate(ps, axis=1)
            pv = _dot(vslct_ref[j * (kb // tk)], p4[0:tk, :])
            for hf in range(1, kb // tk):
                pv = pv + _dot(vslct_ref[j * (kb // tk) + hf], p4[hf * tk:(hf + 1) * tk, :])
            return jnp.concatenate(ms, axis=1), jnp.concatenate(als, axis=1) * acc + pv

        init = (jnp.full((1, hpg * tq), -1e30, F32), jnp.zeros((vr, hpg * tq), F32))
        _, acc_s = lax.fori_loop(0, (r0 + tq + kb - 1) // kb, kv_step, init)
        o_slc = acc_s[0:dh] * (1.0 / jnp.maximum(acc_s[dh:dh + 1], 1e-30))

        out_t = o_cw + g_slc * o_slc
        out_t = jnp.concatenate([out_t[:, sl] for sl in heads], axis=0)
        o_ref[pl.ds(r0, tq), :] = out_t.T.astype(o_ref.dtype)
        return carry

    lax.fori_loop(0, seq // tq, q_tile, 0)


def _nsa_constants(seq):
    nc = seq // STRIDE_CMP
    ns = seq // L_SLC
    cs = np.arange(nc) * STRIDE_CMP
    js = np.arange(ns) * L_SLC
    ovl_t = ((cs[None, :] < js[:, None] + L_SLC) & (cs[None, :] + L_CMP > js[:, None])).astype(np.float32)
    gsel = np.zeros((NSA_G, 16, LANE), np.float32)
    for g in range(NSA_G):
        for h in range(NSA_HPG):
            for j in range(3):
                gsel[g, j * NSA_HPG + h, SM_GATE + (g * NSA_HPG + h) * 3 + j] = 1.0
    qw = NSA_HPG * NSA_DH
    half = ROPE_DIM // 2
    perm = np.zeros((qw, qw), np.float32)
    for l in range(qw):
        d = l % NSA_DH
        if d < ROPE_DIM:
            perm[l + half if d < half else l - half, l] = 1.0
    return jnp.asarray(ovl_t), jnp.asarray(gsel), jnp.asarray(perm, dtype=BF16)


def _nsa(proj, q_blk, kv_blk, sm_blk, bsz, seq, cos_t, sin_t, pe_k, pe_v, wk_cmp, wv_cmp, consts):
    t = bsz * seq
    dh = NSA_DH
    qw = NSA_HPG * dh
    tq, tk = NSA_TQ, NSA_TK
    nc = seq // STRIDE_CMP
    ns = seq // L_SLC
    ovl_t, gsel, perm = consts
    pe = jnp.concatenate([pe_k, pe_v], axis=1)
    zero = jnp.zeros((L_CMP, dh, dh), F32)
    wc = jnp.concatenate([jnp.concatenate([wk_cmp, zero], axis=2),
                          jnp.concatenate([zero, wv_cmp], axis=2)], axis=1)
    wc = wc.reshape(L_CMP * LANE, LANE).astype(BF16)
    kern = functools.partial(_nsa_kernel, seq=seq)
    full = lambda shape: pl.BlockSpec(shape, lambda b, g: tuple(0 for _ in shape))
    return pl.pallas_call(
        kern,
        grid=(bsz, NSA_G),
        in_specs=[pl.BlockSpec((seq, qw), lambda b, g: (b, q_blk + g)),
                  pl.BlockSpec((seq, 3 * LANE), lambda b, g: (b, kv_blk + g)),
                  pl.BlockSpec((seq, LANE), lambda b, g: (b, sm_blk)),
                  pl.BlockSpec((1, seq, LANE), lambda b, g: (b, 0, 0)),
                  pl.BlockSpec((1, seq, LANE), lambda b, g: (b, 0, 0)),
                  full((L_CMP, LANE)),
                  full((L_CMP * LANE, LANE)),
                  full((ns, nc)),
                  pl.BlockSpec((1, 16, LANE), lambda b, g: (g, 0, 0)),
                  full((qw, qw))],
        out_specs=pl.BlockSpec((seq, qw), lambda b, g: (b, g)),
        out_shape=jax.ShapeDtypeStruct((t, NSA_G * qw), BF16),
        scratch_shapes=[pltpu.VMEM((seq // tk, qw, tk), BF16),
                        pltpu.VMEM((seq + L_CMP, LANE), F32),
                        pltpu.VMEM((nc, L_CMP * LANE), F32),
                        pltpu.VMEM((nc, dh), BF16),
                        pltpu.VMEM((dh, nc), BF16),
                        pltpu.VMEM((seq, dh), BF16),
                        pltpu.VMEM((seq // tk, NSA_VR, tk), BF16),
                        pltpu.VMEM((WINDOW + seq, dh), BF16),
                        pltpu.VMEM(((WINDOW + seq) // tk, NSA_VR, tk), BF16),
                        pltpu.VMEM((ns, tq), F32)],
        compiler_params=_params("parallel", "parallel"),
        name="nsa",
    )(proj, proj, proj, cos_t, sin_t, pe, wc, ovl_t, gsel, perm)


def _dot3(a, b):
    a_hi = a.astype(BF16)
    a_lo = (a - a_hi.astype(F32)).astype(BF16)
    b_hi = b.astype(BF16)
    b_lo = (b - b_hi.astype(F32)).astype(BF16)
    d = lambda x, y: jnp.dot(x, y, preferred_element_type=F32)
    return d(a_hi, b_hi) + (d(a_lo, b_hi) + d(a_hi, b_lo))


def _unit_lower_solve(lmats, rhss):
    c, width = rhss[0].shape
    bs = GDN_SOLVE_BLOCK
    done = [[] for _ in rhss]
    for b in range(c // bs):
        r0 = b * bs
        xbs = []
        for lmat, rhs, dn in zip(lmats, rhss, done):
            xb = rhs[r0:r0 + bs]
            if b:
                partial = jnp.concatenate(dn + [jnp.zeros((c - r0, width), F32)], axis=0)
                xb = xb - _dot3(lmat[r0:r0 + bs, :], partial)
            xbs.append(xb)
        for lmat, xb, dn in zip(lmats, xbs, done):
            lb = lmat[r0:r0 + bs, r0:r0 + bs]
            groups = [xb[8 * g:8 * g + 8] for g in range(bs // 8)]
            for j in range(bs - 1):
                row = groups[j // 8][j % 8:j % 8 + 1, :]
                for g in range(j // 8, bs // 8):
                    if 8 * g + 7 > j:
                        groups[g] = groups[g] - lb[8 * g:8 * g + 8, j:j + 1] * row
            dn.append(jnp.concatenate(groups, axis=0))
    return [jnp.concatenate(dn, axis=0) for dn in done]


def _gdn_kernel(qkv_ref, z_ref, sm_ref, cw_ref, alog_ref, dtb_ref, ng_ref, o_ref, buf_ref, xc_ref, st_ref,
                *, k_len):
    c = GDN_CHUNK
    rows = c * GDN_STEP_CHUNKS
    dk, dv = GDN_DK, GDN_DV

    @pl.when(pl.program_id(1) == 0)
    def _():
        buf_ref[0:GDN_HALO, :] = jnp.zeros((GDN_HALO, GDN_QKV), F32)
        st_ref[...] = jnp.zeros(st_ref.shape, F32)

    x = qkv_ref[...]
    buf_ref[GDN_HALO:GDN_HALO + rows, :] = x
    base = GDN_HALO - (k_len - 1)
    acc = jnp.zeros((rows, GDN_QKV), F32)
    for k in range(k_len):
        acc = acc + cw_ref[k:k + 1, :] * buf_ref[base + k:base + k + rows, :]
    buf_ref[0:GDN_HALO, :] = x[rows - GDN_HALO:rows, :]
    xc_ref[...] = acc * jax.nn.sigmoid(acc)

    sm = sm_ref[...]
    beta_all = jax.nn.sigmoid(sm)
    g_all = -jnp.exp(alog_ref[...]) * jax.nn.softplus(sm + dtb_ref[...])
    ri = lax.broadcasted_iota(jnp.int32, (rows, rows), 0)
    rj = lax.broadcasted_iota(jnp.int32, (rows, rows), 1)
    same_chunk_incl = (ri >= rj) & (ri // c == rj // c)
    gc_all = _dot(jnp.where(same_chunk_incl, 1.0, 0.0), g_all, hi=True)
    gc_rows = gc_all.T
    ii = lax.broadcasted_iota(jnp.int32, (c, c), 0)
    jj = lax.broadcasted_iota(jnp.int32, (c, c), 1)
    incl = ii >= jj
    strict = ii > jj

    pre, lmats, rhss = [], [], []
    for n in range(GDN_STEP_CHUNKS):
        rs = slice(n * c, (n + 1) * c)
        for h in range(GDN_H):
            gcol = gc_all[rs, SM_A + h:SM_A + h + 1]
            bcol = beta_all[rs, SM_BETA + h:SM_BETA + h + 1]
            grow = gc_rows[SM_A + h:SM_A + h + 1, rs]
            decay = jnp.exp(jnp.where(incl, gcol - grow, -jnp.inf))
            q = xc_ref[rs, h * dk:(h + 1) * dk]
            k = xc_ref[rs, GDN_H * dk + h * dk:GDN_H * dk + (h + 1) * dk]
            v = xc_ref[rs, 2 * GDN_H * dk + h * dv:2 * GDN_H * dk + (h + 1) * dv]
            q = q * lax.rsqrt(jnp.sum(q * q, axis=-1, keepdims=True) + EPS) * (dk ** -0.5)
            k = k * lax.rsqrt(jnp.sum(k * k, axis=-1, keepdims=True) + EPS)
            eg = jnp.exp(gcol)
            lmat = jnp.where(strict, bcol * _dot(k, k, NT) * decay, 0.0)
            qk = jnp.where(incl, _dot(q, k, NT) * decay, 0.0)
            glast = gcol[c - 1:c, :]
            lmats.append(lmat)
            rhss.append(jnp.concatenate([k * (bcol * eg), v * bcol], axis=1))
            pre.append((q * eg, k * jnp.exp(glast - gcol), qk, jnp.exp(glast)))
    sols = _unit_lower_solve(lmats, rhss)

    z = z_ref[...]
    for n in range(GDN_STEP_CHUNKS):
        rs = slice(n * c, (n + 1) * c)
        idx = range(n * GDN_H, (n + 1) * GDN_H)
        states = [st_ref[h] for h in range(GDN_H)]
        v_news = [sols[i][:, dk:] - _dot(sols[i][:, :dk], st) for i, st in zip(idx, states)]
        o_state = [_dot(pre[i][0], st) for i, st in zip(idx, states)]
        o_intra = [_dot(pre[i][2], vn) for i, vn in zip(idx, v_news)]
        s_add = [_dot(pre[i][1], vn, TN) for i, vn in zip(idx, v_news)]
        for h, i in enumerate(idx):
            st_ref[h] = pre[i][3] * states[h] + s_add[h]
            o = o_state[h] + o_intra[h]
            o = o * lax.rsqrt(jnp.mean(o * o, axis=-1, keepdims=True) + EPS) * ng_ref[...]
            zh = z[rs, h * dv:(h + 1) * dv]
            o_ref[rs, h * dv:(h + 1) * dv] = (o * (zh * jax.nn.sigmoid(zh))).astype(o_ref.dtype)


def _gdn(proj, qkv_blk, z_blk, sm_blk, bsz, seq, conv_w, a_log, dt_bias, norm_g):
    t = bsz * seq
    c = GDN_CHUNK * GDN_STEP_CHUNKS
    nch = seq // c
    k_len = conv_w.shape[0]
    alog = jnp.zeros((1, LANE), F32).at[0, SM_A:SM_A + GDN_H].set(a_log)
    dtb = jnp.zeros((1, LANE), F32).at[0, SM_A:SM_A + GDN_H].set(dt_bias)
    kern = functools.partial(_gdn_kernel, k_len=k_len)
    full = lambda shape: pl.BlockSpec(shape, lambda b, n: tuple(0 for _ in shape))
    return pl.pallas_call(
        kern,
        grid=(bsz, nch),
        in_specs=[pl.BlockSpec((c, GDN_QKV), lambda b, n: (b * nch + n, qkv_blk)),
                  pl.BlockSpec((c, GDN_WIDTH), lambda b, n: (b * nch + n, z_blk)),
                  pl.BlockSpec((c, LANE), lambda b, n: (b * nch + n, sm_blk)),
                  full((k_len, GDN_QKV)), full((1, LANE)), full((1, LANE)), full((1, GDN_DV))],
        out_specs=pl.BlockSpec((c, GDN_WIDTH), lambda b, n: (b * nch + n, 0)),
        out_shape=jax.ShapeDtypeStruct((t, GDN_WIDTH), BF16),
        scratch_shapes=[pltpu.VMEM((GDN_HALO + c, GDN_QKV), F32),
                        pltpu.VMEM((c, GDN_QKV), F32),
                        pltpu.VMEM((GDN_H, GDN_DK, GDN_DV), F32)],
        compiler_params=_params("parallel", "arbitrary"),
        name="gdn",
    )(proj, proj, proj, conv_w, alog, dtb, norm_g.reshape(1, -1))


def _merge_kernel(x_ref, mod_ref, yc_ref, yn_ref, yg_ref, gm_ref, wc_ref, wn_ref, wg_ref, wo_ref, o_ref):
    d = x_ref.shape[1]
    gm = jax.nn.sigmoid(gm_ref[...])
    merged = (gm[:, 0:d] * jnp.dot(yc_ref[...], wc_ref[...], preferred_element_type=F32)
              + gm[:, d:2 * d] * jnp.dot(yn_ref[...], wn_ref[...], preferred_element_type=F32)
              + gm[:, 2 * d:3 * d] * jnp.dot(yg_ref[...], wg_ref[...], preferred_element_type=F32))
    gt = mod_ref[0][2:3]
    o_ref[...] = x_ref[...] + gt * _dot(merged, wo_ref[...])


def _merge(x2, mod_l, y_conv, y_nsa, y_gdn, proj, gm_blk, w_c, w_n, w_g, w_o, seq, tm=512):
    t, d = x2.shape
    per_b = seq // tm
    row = lambda w: pl.BlockSpec((tm, w), lambda i: (i, 0))
    wspec = lambda w: pl.BlockSpec(w.shape, lambda i: (0, 0))
    return pl.pallas_call(
        _merge_kernel,
        grid=(t // tm,),
        in_specs=[row(d),
                  pl.BlockSpec((1, N_MOD, d), lambda i: (i // per_b, 0, 0)),
                  row(y_conv.shape[1]), row(y_nsa.shape[1]), row(y_gdn.shape[1]),
                  pl.BlockSpec((tm, 3 * d), lambda i: (i, gm_blk)),
                  wspec(w_c), wspec(w_n), wspec(w_g), wspec(w_o)],
        out_specs=row(d),
        out_shape=jax.ShapeDtypeStruct((t, d), F32),
        compiler_params=_params("parallel"),
        name="merge",
    )(x2, mod_l, y_conv, y_nsa, y_gdn, proj, w_c, w_n, w_g, w_o)


def _ffn_kernel(x_ref, mod_ref, g_ref, wg_ref, wu_ref, wo_ref, fg_ref, o_ref, *, final):
    m = mod_ref[0]
    x = x_ref[...]
    h = _norm_mod(x, g_ref[...], m[3:4], m[4:5]).astype(BF16)
    gate = jnp.dot(h, wg_ref[...], preferred_element_type=F32)
    up = jnp.dot(h, wu_ref[...], preferred_element_type=F32)
    act = (gate * jax.nn.sigmoid(gate)) * up
    y = x + m[5:6] * _dot(act, wo_ref[...])
    if final:
        y = y * lax.rsqrt(jnp.mean(y * y, axis=-1, keepdims=True) + EPS) * fg_ref[...]
    o_ref[...] = y


def _ffn(x2, mod_l, g, w_in, w_out, final_g, seq, final, tm=512):
    t, d = x2.shape
    hid = w_out.shape[0]
    per_b = seq // tm
    kern = functools.partial(_ffn_kernel, final=final)
    resident = lambda shape, idx: pl.BlockSpec(shape, idx, pipeline_mode=pl.Buffered(1))
    return pl.pallas_call(
        kern,
        grid=(t // tm,),
        in_specs=[pl.BlockSpec((tm, d), lambda i: (i, 0)),
                  pl.BlockSpec((1, N_MOD, d), lambda i: (i // per_b, 0, 0)),
                  pl.BlockSpec((1, d), lambda i: (0, 0)),
                  resident((d, hid), lambda i: (0, 0)),
                  resident((d, hid), lambda i: (0, 1)),
                  resident((hid, d), lambda i: (0, 0)),
                  pl.BlockSpec((1, d), lambda i: (0, 0))],
        out_specs=pl.BlockSpec((tm, d), lambda i: (i, 0)),
        out_shape=jax.ShapeDtypeStruct((t, d), F32),
        compiler_params=_params("parallel"),
        name="ffn",
    )(x2, mod_l, g, w_in, w_in, w_out, final_g)


def _in_layout(d):
    segs = {}
    off = 0
    for name, width in (("gm", 3 * d), ("conv", 2 * C_CONV), ("q", NSA_WIDTH), ("qkv", GDN_QKV),
                        ("z", GDN_WIDTH), ("small", LANE), ("pad", LANE),
                        ("kv", 6 * NSA_G * NSA_DH)):
        segs[name] = off
        off += width
    return segs, off


def _rearrange_in_proj(w_in, b_in, d):
    splits = (2 * C_CONV, NSA_WIDTH, 6 * NSA_G * NSA_DH, 3 * NSA_HEADS, GDN_QKV, GDN_WIDTH, GDN_H, GDN_H, 3 * d)
    offs = np.concatenate([[0], np.cumsum(splits)])
    names = ("conv", "q", "kv", "gnsa", "qkv", "z", "beta", "a", "gm")
    segs, total = _in_layout(d)

    def arrange(a):
        src = {n: a[..., int(offs[i]):int(offs[i + 1])] for i, n in enumerate(names)}
        lead = a.shape[:-1]
        kv = jnp.swapaxes(src["kv"].reshape(lead + (6, NSA_G, NSA_DH)), -3, -2).reshape(lead + (-1,))
        small_pad = jnp.zeros(lead + (LANE - 3 * NSA_HEADS - 2 * GDN_H,), a.dtype)
        pad = jnp.zeros(lead + (LANE,), a.dtype)
        out = jnp.concatenate([src["gm"], src["conv"], src["q"], src["qkv"], src["z"],
                               src["gnsa"], src["beta"], src["a"], small_pad, pad, kv], axis=-1)
        assert out.shape[-1] == total
        return out

    return arrange(w_in.astype(BF16)), arrange(b_in)[:, None, :], segs


def kernel(x, c, positions, ada_w, ada_b, norm_mix_g, norm_ffn_g, w_in, b_in, conv_dw_w, conv_dw_b, conv_ln_g, conv_ln_b, nsa_pe_k, nsa_pe_v, nsa_wk_cmp, nsa_wv_cmp, gdn_conv_w, gdn_a_log, gdn_dt_bias, gdn_norm_g, w_up_conv, w_up_nsa, w_up_gdn, w_o, ffn_w_in, ffn_w_out, final_norm_g):
    bsz, seq, d = x.shape
    depth = ada_w.shape[0]
    t = bsz * seq

    mod = _modulation(c, ada_w, ada_b).reshape(depth, bsz, N_MOD, d)
    cos_t, sin_t = _rope_tables(positions)
    nsa_consts = _nsa_constants(seq)
    w_in_r, b_in_r, segs = _rearrange_in_proj(w_in, b_in, d)
    blk = lambda name, width: segs[name] // width
    qw = NSA_HPG * NSA_DH

    x2 = x.reshape(t, d)
    for l in range(depth):
        proj = _in_projection(x2, mod[l], norm_mix_g[l].reshape(1, d), w_in_r, b_in_r, l, seq)
        y_conv = _conformer_conv(proj, blk("conv", 2 * C_CONV), bsz, seq,
                                 conv_dw_w[l], conv_dw_b[l], conv_ln_g[l], conv_ln_b[l])
        y_nsa = _nsa(proj, blk("q", qw), blk("kv", 3 * LANE), blk("small", LANE), bsz, seq, cos_t, sin_t,
                     nsa_pe_k[l], nsa_pe_v[l], nsa_wk_cmp[l], nsa_wv_cmp[l], nsa_consts)
        y_gdn = _gdn(proj, blk("qkv", GDN_QKV), blk("z", GDN_WIDTH), blk("small", LANE), bsz, seq,
                     gdn_conv_w[l], gdn_a_log[l], gdn_dt_bias[l], gdn_norm_g[l])
        x2 = _merge(x2, mod[l], y_conv, y_nsa, y_gdn, proj, blk("gm", 3 * d),
                    w_up_conv[l].astype(BF16), w_up_nsa[l].astype(BF16), w_up_gdn[l].astype(BF16),
                    w_o[l].astype(BF16), seq)
        x2 = _ffn(x2, mod[l], norm_ffn_g[l].reshape(1, d), ffn_w_in[l].astype(BF16),
                  ffn_w_out[l].astype(BF16), final_norm_g.reshape(1, d), seq, final=(l == depth - 1))
    return x2.reshape(bsz, seq, d)
```
